```python
import math
import jax, jax.numpy as jnp
from jax import lax
import numpy as np

D_MODEL = 1024
BATCH = 8
SEQ = 8192
DEPTH = 2
DEC_BATCH = 8
DEC_SEQ = 2048
PAST_LEN = 128

N_DIFF_HEADS = 4
DIFF_QK_DIM = 64
DIFF_V_DIM = 128
N_MLA_HEADS = 4
MLA_Q_RANK = 256
MLA_KV_RANK = 256
MLA_NOPE_DIM = 64
MLA_ROPE_DIM = 32
MLA_QK_DIM = MLA_NOPE_DIM + MLA_ROPE_DIM
MLA_V_DIM = 128
ROPE_THETA = 10000.0
DIFF_Q_W = N_DIFF_HEADS * 2 * DIFF_QK_DIM
DIFF_V_W = N_DIFF_HEADS * DIFF_V_DIM
MIX_WIDTH = DIFF_V_W + N_MLA_HEADS * MLA_V_DIM
IN_SPLITS = [DIFF_Q_W, 2 * DIFF_Q_W, 2 * DIFF_Q_W + DIFF_V_W,
             2 * DIFF_Q_W + DIFF_V_W + MLA_Q_RANK,
             2 * DIFF_Q_W + DIFF_V_W + MLA_Q_RANK + MLA_KV_RANK]
IN_WIDTH = IN_SPLITS[-1] + MLA_ROPE_DIM
PEER_HEADS = 8
PEER_N_KEYS = 128
PEER_N_EXPERTS = PEER_N_KEYS * PEER_N_KEYS
PEER_KEY_DIM = 256
PEER_HALF = PEER_KEY_DIM // 2
PEER_TOPK = 16
TOKEN_CHUNK = 128
Q_BLOCK = 128
EPS = 1e-6

kernel_name = "hybrid_diffattn_mla_peer_encoder"


def rmsnorm(x, g):
    x32 = x.astype(jnp.float32)
    y = x32 * lax.rsqrt(jnp.mean(x32 * x32, axis=-1, keepdims=True) + EPS)
    return (y * g.astype(jnp.float32)).astype(x.dtype)


def rope(x, pos):
    half = x.shape[-1] // 2
    inv = ROPE_THETA ** (-jnp.arange(half, dtype=jnp.float32) / half)
    ang = pos.astype(jnp.float32)[:, None] * inv[None, :]
    cos = jnp.cos(ang)[None, :, None, :]
    sin = jnp.sin(ang)[None, :, None, :]
    x1 = x[..., :half].astype(jnp.float32)
    x2 = x[..., half:].astype(jnp.float32)
    return jnp.concatenate([x1 * cos - x2 * sin, x2 * cos + x1 * sin], axis=-1).astype(x.dtype)


def to_blocks(q):
    b, s = q.shape[:2]
    return jnp.moveaxis(q.reshape((b, s // Q_BLOCK, Q_BLOCK) + q.shape[2:]), 1, 0)


def from_blocks(o):
    nb, b, qb = o.shape[:3]
    return jnp.moveaxis(o, 0, 1).reshape((b, nb * qb) + o.shape[3:])


def alibi_slopes(n):
    return 2.0 ** (-8.0 * jnp.arange(1, n + 1, dtype=jnp.float32) / n)


def diff_attention(q, k, v, lam, slopes):
    s_len = k.shape[1]
    pos = jnp.arange(s_len)
    scale = DIFF_QK_DIM ** -0.5

    def block(args):
        qb, pq = args
        s = jnp.einsum('bqhjd,bshjd->bhjqs', qb, k).astype(jnp.float32) * scale
        dist = jnp.abs(pq[:, None] - pos[None, :]).astype(jnp.float32)
        s = s - slopes[None, :, None, None, None] * dist[None, None, None]
        p = jax.nn.softmax(s, axis=-1)
        a = p[:, :, 0] - lam * p[:, :, 1]
        return jnp.einsum('bhqs,bshe->bqhe', a.astype(v.dtype), v)

    o = lax.map(block, (to_blocks(q), pos.reshape(-1, Q_BLOCK)))
    return from_blocks(o)


def mla_attention(q, k, v):
    scale = MLA_QK_DIM ** -0.5

    def block(qb):
        s = jnp.einsum('bqhd,bshd->bhqs', qb, k).astype(jnp.float32) * scale
        p = jax.nn.softmax(s, axis=-1)
        return jnp.einsum('bhqs,bshe->bqhe', p.astype(v.dtype), v)

    return from_blocks(lax.map(block, to_blocks(q)))


def peer(x, w_q, key1, key2, u, v):
    b, s, d = x.shape
    xt = x.reshape(-1, TOKEN_CHUNK, d)

    def chunk(xc):
        c = xc.shape[0]
        q = (xc @ w_q).reshape(c, PEER_HEADS, 2, PEER_HALF)
        s1 = jnp.einsum('chd,hkd->chk', q[:, :, 0], key1).astype(jnp.float32)
        s2 = jnp.einsum('chd,hkd->chk', q[:, :, 1], key2).astype(jnp.float32)
        v1, i1 = lax.top_k(s1, PEER_TOPK)
        v2, i2 = lax.top_k(s2, PEER_TOPK)
        cand = (v1[..., :, None] + v2[..., None, :]).reshape(c, PEER_HEADS, PEER_TOPK * PEER_TOPK)
        cidx = (i1[..., :, None] * PEER_N_KEYS + i2[..., None, :]).reshape(c, PEER_HEADS, PEER_TOPK * PEER_TOPK)
        sc, sel = lax.top_k(cand, PEER_TOPK)
        eidx = jnp.take_along_axis(cidx, sel, axis=-1)
        g = jax.nn.softmax(sc, axis=-1)
        ue = u[eidx]
        h = jax.nn.gelu(jnp.einsum('chkd,cd->chk', ue, xc).astype(jnp.float32), approximate=False)
        ve = v[eidx]
        return jnp.einsum('chk,chkd->cd', (g * h).astype(xc.dtype), ve)

    return lax.map(chunk, xt).reshape(b, s, d)


def layer(x, l, norm_mix_g, w_in, diff_q_norm_g, diff_k_norm_g, lam_q1, lam_k1, lam_q2, lam_k2,
          diff_subln_g, mla_q_latent_g, mla_w_uq, mla_kv_latent_g, mla_w_ukv, mla_q_norm_g,
          mla_k_norm_g, w_out, norm_ffn_g, peer_w_q, peer_key1, peer_key2, peer_u, peer_v):
    b, s, _ = x.shape
    pos = jnp.arange(s)
    h = rmsnorm(x, norm_mix_g) @ w_in
    dq, dk, dv, cq, ckv, kr = jnp.split(h, IN_SPLITS, axis=-1)

    dq = rmsnorm(dq.reshape(b, s, N_DIFF_HEADS, 2, DIFF_QK_DIM), diff_q_norm_g)
    dk = rmsnorm(dk.reshape(b, s, N_DIFF_HEADS, 2, DIFF_QK_DIM), diff_k_norm_g)
    dv = dv.reshape(b, s, N_DIFF_HEADS, DIFF_V_DIM)
    lam_init = 0.8 - 0.6 * math.exp(-0.3 * l)
    lam = (jnp.exp(jnp.sum(lam_q1.astype(jnp.float32) * lam_k1.astype(jnp.float32)))
           - jnp.exp(jnp.sum(lam_q2.astype(jnp.float32) * lam_k2.astype(jnp.float32))) + lam_init)
    od = diff_attention(dq, dk, dv, lam, alibi_slopes(N_DIFF_HEADS))
    od = (rmsnorm(od, diff_subln_g) * (1.0 - lam_init)).reshape(b, s, DIFF_V_W)

    mq = (rmsnorm(cq, mla_q_latent_g) @ mla_w_uq).reshape(b, s, N_MLA_HEADS, MLA_QK_DIM)
    kv = (rmsnorm(ckv, mla_kv_latent_g) @ mla_w_ukv).reshape(b, s, N_MLA_HEADS, MLA_NOPE_DIM + MLA_V_DIM)
    k_nope, mv = kv[..., :MLA_NOPE_DIM], kv[..., MLA_NOPE_DIM:]
    k_rope = jnp.broadcast_to(kr[:, :, None, :], (b, s, N_MLA_HEADS, MLA_ROPE_DIM))
    mk = jnp.concatenate([k_nope, k_rope], axis=-1)
    mq = rmsnorm(mq, mla_q_norm_g)
    mk = rmsnorm(mk, mla_k_norm_g)
    mq = jnp.concatenate([mq[..., :MLA_NOPE_DIM], rope(mq[..., MLA_NOPE_DIM:], pos)], axis=-1)
    mk = jnp.concatenate([mk[..., :MLA_NOPE_DIM], rope(mk[..., MLA_NOPE_DIM:], pos)], axis=-1)
    om = mla_attention(mq, mk, mv).reshape(b, s, N_MLA_HEADS * MLA_V_DIM)

    x = x + jnp.concatenate([od, om], axis=-1) @ w_out
    x = x + peer(rmsnorm(x, norm_ffn_g), peer_w_q, peer_key1, peer_key2, peer_u, peer_v)
    return x


def setup_inputs(seed: int = 0) -> dict:
    key = jax.random.key(seed)
    ks = jax.random.split(key, 26)
    f = jnp.float32

    def nrm(k, shape, scale):
        return jax.random.normal(k, shape, f) * scale

    def gain(k, shape):
        return 1.0 + 0.02 * jax.random.normal(k, shape, f)

    L, D = DEPTH, D_MODEL
    return {
        "x_prompt": nrm(ks[0], (BATCH, SEQ, D), 1.0),
        "x_sample": nrm(ks[1], (DEC_BATCH, DEC_SEQ, D), 1.0),
        "norm_mix_g": gain(ks[2], (L, D)),
        "w_in": nrm(ks[3], (L, D, IN_WIDTH), D ** -0.5),
        "diff_q_norm_g": gain(ks[4], (L, DIFF_QK_DIM)),
        "diff_k_norm_g": gain(ks[5], (L, DIFF_QK_DIM)),
        "lam_q1": nrm(ks[6], (L, DIFF_QK_DIM), 0.1),
        "lam_k1": nrm(ks[7], (L, DIFF_QK_DIM), 0.1),
        "lam_q2": nrm(ks[8], (L, DIFF_QK_DIM), 0.1),
        "lam_k2": nrm(ks[9], (L, DIFF_QK_DIM), 0.1),
        "diff_subln_g": gain(ks[10], (L, DIFF_V_DIM)),
        "mla_q_latent_g": gain(ks[11], (L, MLA_Q_RANK)),
        "mla_w_uq": nrm(ks[12], (L, MLA_Q_RANK, N_MLA_HEADS * MLA_QK_DIM), MLA_Q_RANK ** -0.5),
        "mla_kv_latent_g": gain(ks[13], (L, MLA_KV_RANK)),
        "mla_w_ukv": nrm(ks[14], (L, MLA_KV_RANK, N_MLA_HEADS * (MLA_NOPE_DIM + MLA_V_DIM)), MLA_KV_RANK ** -0.5),
        "mla_q_norm_g": gain(ks[15], (L, MLA_QK_DIM)),
        "mla_k_norm_g": gain(ks[16], (L, MLA_QK_DIM)),
        "w_out": nrm(ks[17], (L, MIX_WIDTH, D), MIX_WIDTH ** -0.5),
        "norm_ffn_g": gain(ks[18], (L, D)),
        "peer_w_q": nrm(ks[19], (L, D, PEER_HEADS * PEER_KEY_DIM), D ** -0.5),
        "peer_key1": nrm(ks[20], (L, PEER_HEADS, PEER_N_KEYS, PEER_HALF), PEER_HALF ** -0.5),
        "peer_key2": nrm(ks[21], (L, PEER_HEADS, PEER_N_KEYS, PEER_HALF), PEER_HALF ** -0.5),
        "peer_u": nrm(ks[22], (L, PEER_N_EXPERTS, D), D ** -0.5),
        "peer_v": nrm(ks[23], (L, PEER_N_EXPERTS, D), D ** -0.5),
    }


def reference(x_prompt, x_sample, norm_mix_g, w_in, diff_q_norm_g, diff_k_norm_g, lam_q1, lam_k1,
              lam_q2, lam_k2, diff_subln_g, mla_q_latent_g, mla_w_uq, mla_kv_latent_g, mla_w_ukv,
              mla_q_norm_g, mla_k_norm_g, w_out, norm_ffn_g, peer_w_q, peer_key1, peer_key2,
              peer_u, peer_v):
    y_prompt = x_prompt
    y_sample = x_sample
    for l in range(DEPTH):
        p = (norm_mix_g[l], w_in[l], diff_q_norm_g[l], diff_k_norm_g[l], lam_q1[l], lam_k1[l],
             lam_q2[l], lam_k2[l], diff_subln_g[l], mla_q_latent_g[l], mla_w_uq[l],
             mla_kv_latent_g[l], mla_w_ukv[l], mla_q_norm_g[l], mla_k_norm_g[l], w_out[l],
             norm_ffn_g[l], peer_w_q[l], peer_key1[l], peer_key2[l], peer_u[l], peer_v[l])
        y_prompt = layer(y_prompt, l, *p)
        y_sample = layer(y_sample, l, *p)
    return (y_prompt, y_sample)
```

```python
import functools
import math

import jax
import jax.numpy as jnp
from jax import lax
from jax.experimental import pallas as pl
from jax.experimental.pallas import tpu as pltpu

F32 = jnp.float32
BF16 = jnp.bfloat16

D_MODEL = 1024
DEPTH = 2
N_DIFF_HEADS = 4
DIFF_QK_DIM = 64
DIFF_V_DIM = 128
N_MLA_HEADS = 4
MLA_Q_RANK = 256
MLA_KV_RANK = 256
MLA_NOPE_DIM = 64
MLA_ROPE_DIM = 32
MLA_QK_DIM = MLA_NOPE_DIM + MLA_ROPE_DIM
MLA_V_DIM = 128
ROPE_THETA = 10000.0
DIFF_Q_W = N_DIFF_HEADS * 2 * DIFF_QK_DIM
DIFF_V_W = N_DIFF_HEADS * DIFF_V_DIM
MLA_V_W = N_MLA_HEADS * MLA_V_DIM
IN_WIDTH = 3 * DIFF_Q_W + MLA_Q_RANK + MLA_KV_RANK + MLA_ROPE_DIM
PEER_HEADS = 8
PEER_N_KEYS = 128
PEER_N_EXPERTS = PEER_N_KEYS * PEER_N_KEYS
PEER_HALF = 128
PEER_TOPK = 16
EPS = 1e-6

HEAD_LANES = 128
V7X_VMEM_LIMIT = 56 * 1024 * 1024

_OFF_DQ = 0
_OFF_DK = DIFF_Q_W
_OFF_DV = 2 * DIFF_Q_W
_OFF_CQ = 3 * DIFF_Q_W
_OFF_CKV = _OFF_CQ + MLA_Q_RANK
_OFF_KR = _OFF_CKV + MLA_KV_RANK


def _nt_dot(a, b):
    return lax.dot_general(a, b, (((1,), (1,)), ((), ())), preferred_element_type=F32)


def _dot(a, b):
    return jnp.dot(a, b, preferred_element_type=F32)


def _group_rmsnorm_fm(a, n_groups, width, g_col):
    t = a.shape[1]
    a3 = a.reshape(n_groups, width, t)
    ms = jnp.mean(a3 * a3, axis=1, keepdims=True)
    return (a3 * lax.rsqrt(ms + EPS)).reshape(n_groups * width, t) * g_col


def _rope_fm(r, cos, sin):
    half = MLA_ROPE_DIM // 2
    r1, r2 = r[:half], r[half:]
    return r1 * cos - r2 * sin, r2 * cos + r1 * sin


def _pre_attn_kernel(x_ref, gmix_ref, wint_ref, gq_ref, gk_ref, gcq_ref, wuqt_ref, gckv_ref,
                     wukvt_ref, gmq_ref, gmk_ref, cos_ref, sin_ref,
                     qd_ref, kd_ref, vd_ref, qm_ref, km_ref, vm_ref, kt_scr):
    x = x_ref[0]
    t = x.shape[0]
    ms = jnp.mean(x * x, axis=-1, keepdims=True)
    xn = (x * lax.rsqrt(ms + EPS) * gmix_ref[...]).astype(BF16)
    ht = _nt_dot(wint_ref[...], xn)
    cos = cos_ref[...]
    sin = sin_ref[...]

    dq = _group_rmsnorm_fm(ht[_OFF_DQ:_OFF_DQ + DIFF_Q_W], 2 * N_DIFF_HEADS, DIFF_QK_DIM, gq_ref[...])
    qd_ref[0] = dq.astype(BF16)
    dk = _group_rmsnorm_fm(ht[_OFF_DK:_OFF_DK + DIFF_Q_W], 2 * N_DIFF_HEADS, DIFF_QK_DIM, gk_ref[...])
    kd_ref[0] = dk.T.astype(BF16)
    vd_ref[0] = ht[_OFF_DV:_OFF_DV + DIFF_V_W].astype(BF16)

    cq = ht[_OFF_CQ:_OFF_CQ + MLA_Q_RANK]
    cqn = (cq * lax.rsqrt(jnp.mean(cq * cq, axis=0, keepdims=True) + EPS) * gcq_ref[...]).astype(BF16)
    mq = _dot(wuqt_ref[...], cqn)
    pad = jnp.zeros((HEAD_LANES - MLA_QK_DIM, t), BF16)
    for h in range(N_MLA_HEADS):
        m = mq[h * MLA_QK_DIM:(h + 1) * MLA_QK_DIM]
        mn = m * lax.rsqrt(jnp.mean(m * m, axis=0, keepdims=True) + EPS) * gmq_ref[...]
        o1, o2 = _rope_fm(mn[MLA_NOPE_DIM:], cos, sin)
        base = h * HEAD_LANES
        qm_ref[0, base:base + MLA_NOPE_DIM] = mn[:MLA_NOPE_DIM].astype(BF16)
        qm_ref[0, base + MLA_NOPE_DIM:base + MLA_NOPE_DIM + 16] = o1.astype(BF16)
        qm_ref[0, base + MLA_NOPE_DIM + 16:base + MLA_QK_DIM] = o2.astype(BF16)
        qm_ref[0, base + MLA_QK_DIM:base + HEAD_LANES] = pad

    ckv = ht[_OFF_CKV:_OFF_CKV + MLA_KV_RANK]
    ckvn = (ckv * lax.rsqrt(jnp.mean(ckv * ckv, axis=0, keepdims=True) + EPS) * gckv_ref[...]).astype(BF16)
    kv = _dot(wukvt_ref[...], ckvn)
    kr = ht[_OFF_KR:_OFF_KR + MLA_ROPE_DIM]
    kr_ss = jnp.sum(kr * kr, axis=0, keepdims=True)
    gmk = gmk_ref[...]
    per_head = MLA_NOPE_DIM + MLA_V_DIM
    for h in range(N_MLA_HEADS):
        kn = kv[h * per_head:h * per_head + MLA_NOPE_DIM]
        v = kv[h * per_head + MLA_NOPE_DIM:(h + 1) * per_head]
        ms_k = (jnp.sum(kn * kn, axis=0, keepdims=True) + kr_ss) * (1.0 / MLA_QK_DIM)
        inv = lax.rsqrt(ms_k + EPS)
        o1, o2 = _rope_fm(kr * inv * gmk[MLA_NOPE_DIM:], cos, sin)
        base = h * HEAD_LANES
        kt_scr[base:base + MLA_NOPE_DIM] = kn * inv * gmk[:MLA_NOPE_DIM]
        kt_scr[base + MLA_NOPE_DIM:base + MLA_NOPE_DIM + 16] = o1
        kt_scr[base + MLA_NOPE_DIM + 16:base + MLA_QK_DIM] = o2
        kt_scr[base + MLA_QK_DIM:base + HEAD_LANES] = jnp.zeros((HEAD_LANES - MLA_QK_DIM, t), F32)
        vm_ref[0, base:base + HEAD_LANES] = v.astype(BF16)
    km_ref[0] = kt_scr[...].T.astype(BF16)


def _pre_attn(x, p, cos_t, sin_t, *, tile):
    b, s, d = x.shape
    hw = N_DIFF_HEADS * HEAD_LANES
    full = lambda a: pl.BlockSpec(a.shape, lambda bi, i: (0,) * a.ndim)
    fm_spec = pl.BlockSpec((1, hw, tile), lambda bi, i: (bi, 0, i))
    tm_spec = pl.BlockSpec((1, tile, hw), lambda bi, i: (bi, i, 0))
    weights = (p["gmix"], p["w_int"], p["gq"], p["gk"], p["gcq"], p["w_uqt"], p["gckv"], p["w_ukvt"],
               p["gmq"], p["gmk"])
    rope_spec = pl.BlockSpec((MLA_ROPE_DIM // 2, tile), lambda bi, i: (0, i))
    fm = jax.ShapeDtypeStruct((b, hw, s), BF16)
    tm = jax.ShapeDtypeStruct((b, s, hw), BF16)
    return pl.pallas_call(
        _pre_attn_kernel,
        grid=(b, s // tile),
        in_specs=[pl.BlockSpec((1, tile, d), lambda bi, i: (bi, i, 0))] + [full(w) for w in weights]
        + [rope_spec, rope_spec],
        out_specs=[fm_spec, tm_spec, fm_spec, fm_spec, tm_spec, fm_spec],
        out_shape=[fm, tm, fm, fm, tm, fm],
        scratch_shapes=[pltpu.VMEM((hw, tile), F32)],
        compiler_params=pltpu.CompilerParams(
            dimension_semantics=("parallel", "parallel"), vmem_limit_bytes=V7X_VMEM_LIMIT),
        name="pre_attn",
    )(x, *weights, cos_t, sin_t)


def _online_softmax_step(k, vt, qp, bias, m, l, acc_ref):
    s = _dot(k, qp)
    if bias is not None:
        s = s - bias
    mn = jnp.maximum(m, jnp.max(s, axis=0, keepdims=True))
    alpha = jnp.exp(m - mn)
    p = jnp.exp(s - mn)
    l = alpha * l + jnp.sum(p, axis=0, keepdims=True)
    acc_ref[...] = alpha * acc_ref[...] + _dot(vt, p.astype(BF16))
    return mn, l


def _diff_attn_kernel(scal_ref, qt_ref, k_ref, vt_ref, g_ref, o_ref, acc0_ref, acc1_ref, *, tq, tk, n_kv):
    h = pl.program_id(1)
    i = pl.program_id(2)
    lam = scal_ref[0]
    sub_scale = scal_ref[1]
    slope = scal_ref[2 + h]
    q = qt_ref[0]
    row = lax.broadcasted_iota(jnp.int32, q.shape, 0)
    zero = jnp.zeros_like(q)
    q0 = jnp.where(row < DIFF_QK_DIM, q, zero)
    q1 = jnp.where(row >= DIFF_QK_DIM, q, zero)
    d0 = (i * tq + lax.broadcasted_iota(jnp.int32, (tk, tq), 1)
          - lax.broadcasted_iota(jnp.int32, (tk, tq), 0)).astype(F32)
    acc0_ref[...] = jnp.zeros_like(acc0_ref)
    acc1_ref[...] = jnp.zeros_like(acc1_ref)

    def body(j, carry):
        m0, l0, m1, l1 = carry
        off = pl.multiple_of(j * tk, tk)
        k = k_ref[0, pl.ds(off, tk), :]
        vt = vt_ref[0, :, pl.ds(off, tk)]
        bias = slope * jnp.abs(d0 - (j * tk).astype(F32))
        m0, l0 = _online_softmax_step(k, vt, q0, bias, m0, l0, acc0_ref)
        m1, l1 = _online_softmax_step(k, vt, q1, bias, m1, l1, acc1_ref)
        return m0, l0, m1, l1

    neg = jnp.full((1, tq), -jnp.inf, F32)
    zer = jnp.zeros((1, tq), F32)
    _, l0, _, l1 = lax.fori_loop(0, n_kv, body, (neg, zer, neg, zer))
    o = acc0_ref[...] / l0 - lam * (acc1_ref[...] / l1)
    o = o * lax.rsqrt(jnp.mean(o * o, axis=0, keepdims=True) + EPS) * g_ref[...] * sub_scale
    o_ref[0] = o.T.astype(BF16)


def _mla_attn_kernel(qt_ref, k_ref, vt_ref, o_ref, acc_ref, *, tq, tk, n_kv):
    q = qt_ref[0]
    acc_ref[...] = jnp.zeros_like(acc_ref)

    def body(j, carry):
        m, l = carry
        off = pl.multiple_of(j * tk, tk)
        k = k_ref[0, pl.ds(off, tk), :]
        vt = vt_ref[0, :, pl.ds(off, tk)]
        return _online_softmax_step(k, vt, q, None, m, l, acc_ref)

    _, l = lax.fori_loop(0, n_kv, body, (jnp.full((1, tq), -jnp.inf, F32), jnp.zeros((1, tq), F32)))
    o_ref[0] = (acc_ref[...] / l).T.astype(BF16)


def _attn_specs(s, tq):
    q_spec = pl.BlockSpec((1, HEAD_LANES, tq), lambda bi, h, i: (bi, h, i))
    k_spec = pl.BlockSpec((1, s, HEAD_LANES), lambda bi, h, i: (bi, 0, h))
    v_spec = pl.BlockSpec((1, HEAD_LANES, s), lambda bi, h, i: (bi, h, 0))
    o_spec = pl.BlockSpec((1, tq, HEAD_LANES), lambda bi, h, i: (bi, i, h))
    return q_spec, k_spec, v_spec, o_spec


def _diff_attn(scal, qt, k, vt, g_col, *, tq, tk):
    b, hw, s = qt.shape
    q_spec, k_spec, v_spec, o_spec = _attn_specs(s, tq)
    return pl.pallas_call(
        functools.partial(_diff_attn_kernel, tq=tq, tk=tk, n_kv=s // tk),
        grid=(b, N_DIFF_HEADS, s // tq),
        in_specs=[pl.BlockSpec(memory_space=pltpu.SMEM), q_spec, k_spec, v_spec,
                  pl.BlockSpec(g_col.shape, lambda bi, h, i: (0, 0))],
        out_specs=o_spec,
        out_shape=jax.ShapeDtypeStruct((b, s, hw), BF16),
        scratch_shapes=[pltpu.VMEM((HEAD_LANES, tq), F32), pltpu.VMEM((HEAD_LANES, tq), F32)],
        compiler_params=pltpu.CompilerParams(
            dimension_semantics=("parallel", "parallel", "parallel"), vmem_limit_bytes=V7X_VMEM_LIMIT),
        name="diff_attn",
    )(scal, qt, k, vt, g_col)


def _mla_attn(qt, k, vt, *, tq, tk):
    b, hw, s = qt.shape
    q_spec, k_spec, v_spec, o_spec = _attn_specs(s, tq)
    return pl.pallas_call(
        functools.partial(_mla_attn_kernel, tq=tq, tk=tk, n_kv=s // tk),
        grid=(b, N_MLA_HEADS, s // tq),
        in_specs=[q_spec, k_spec, v_spec],
        out_specs=o_spec,
        out_shape=jax.ShapeDtypeStruct((b, s, hw), BF16),
        scratch_shapes=[pltpu.VMEM((HEAD_LANES, tq), F32)],
        compiler_params=pltpu.CompilerParams(
            dimension_semantics=("parallel", "parallel", "parallel"), vmem_limit_bytes=V7X_VMEM_LIMIT),
        name="mla_attn",
    )(qt, k, vt)


def _post_attn_kernel(x_ref, od_ref, om_ref, wout_ref, gffn_ref, wqt_ref, keys_ref,
                      x1_ref, xn_ref, sc_ref):
    x1 = (x_ref[0] + _dot(od_ref[0], wout_ref[:DIFF_V_W]) + _dot(om_ref[0], wout_ref[DIFF_V_W:]))
    x1_ref[0] = x1
    ms = jnp.mean(x1 * x1, axis=-1, keepdims=True)
    xn = (x1 * lax.rsqrt(ms + EPS) * gffn_ref[...]).astype(BF16)
    xn_ref[0] = xn
    qt = _nt_dot(wqt_ref[...], xn).astype(BF16)
    for hs in range(2 * PEER_HEADS):
        sc_ref[0, hs] = _dot(keys_ref[hs], qt[hs * PEER_HALF:(hs + 1) * PEER_HALF])


def _post_attn(x, od, om, p, *, tile):
    b, s, d = x.shape
    full = lambda a: pl.BlockSpec(a.shape, lambda bi, i: (0,) * a.ndim)
    tok = lambda w: pl.BlockSpec((1, tile, w), lambda bi, i: (bi, i, 0))
    weights = (p["w_out"], p["gffn"], p["w_qt"], p["keys"])
    return pl.pallas_call(
        _post_attn_kernel,
        grid=(b, s // tile),
        in_specs=[tok(d), tok(DIFF_V_W), tok(MLA_V_W)] + [full(w) for w in weights],
        out_specs=[tok(d), tok(d),
                   pl.BlockSpec((1, 2 * PEER_HEADS, PEER_N_KEYS, tile), lambda bi, i: (bi, 0, 0, i))],
        out_shape=[jax.ShapeDtypeStruct((b, s, d), F32), jax.ShapeDtypeStruct((b, s, d), BF16),
                   jax.ShapeDtypeStruct((b, 2 * PEER_HEADS, PEER_N_KEYS, s), F32)],
        compiler_params=pltpu.CompilerParams(
            dimension_semantics=("parallel", "parallel"), vmem_limit_bytes=V7X_VMEM_LIMIT),
        name="post_attn",
    )(x, od, om, *weights)


_SUB = 8


def _top16_ranked(s, sub_idx, row16):
    rank = jnp.full(s.shape, float(PEER_N_KEYS - 1), F32)
    stacked = jnp.zeros(row16.shape, F32)
    vals = []
    for k in range(PEER_TOPK):
        m = jnp.max(s, axis=0, keepdims=True)
        first = jnp.min(jnp.where(s == m, sub_idx, float(PEER_N_KEYS)), axis=0, keepdims=True)
        sel = sub_idx == first
        rank = jnp.where(sel, float(k), rank)
        s = jnp.where(sel, -jnp.inf, s)
        stacked = jnp.where(row16 == float(k), m, stacked)
        vals.append(m)
    return vals, stacked, rank


def _peer_topk_kernel(sc_ref, cnt_ref, e1_ref, r2_ref, e2_ref, *, td):
    sub_idx = lax.broadcasted_iota(jnp.int32, (PEER_N_KEYS, td), 0).astype(F32)
    row16 = lax.broadcasted_iota(jnp.int32, (PEER_TOPK, td), 0).astype(F32)
    row8 = lax.broadcasted_iota(jnp.int32, (_SUB, td), 0).astype(F32)
    neg_inf = jnp.full((_SUB, td), -jnp.inf, F32)

    def head(h, carry):
        s1 = sc_ref[0, 2 * h]
        s2 = sc_ref[0, 2 * h + 1]
        v1, _, rank1 = _top16_ranked(s1, sub_idx, row16)
        v2, v2s, rank2 = _top16_ranked(s2, sub_idx, row16)

        cells, flats, n_invalid = [], [], []
        cells.append(v1[0] + v2s[:_SUB]); flats.append(row8); n_invalid.append(0)
        cells.append(v1[0] + v2s[_SUB:]); flats.append(row8 + float(_SUB)); n_invalid.append(0)
        for a in range(1, PEER_TOPK):
            nb = PEER_TOPK // (a + 1)
            cells.append(jnp.where(row8 < float(nb), v1[a] + v2s[:_SUB], neg_inf))
            flats.append(row8 + float(a * PEER_TOPK))
            n_invalid.append(_SUB - nb)
        big = float(PEER_TOPK * PEER_TOPK)
        top = None
        z = jnp.zeros((1, td), F32)
        for _ in range(PEER_TOPK):
            m = jnp.max(functools.reduce(jnp.maximum, cells), axis=0, keepdims=True)
            cand = [jnp.where(c == m, f, big) for c, f in zip(cells, flats)]
            first = jnp.min(functools.reduce(jnp.minimum, cand), axis=0, keepdims=True)
            cells = [jnp.where(f == first, -jnp.inf, c) for c, f in zip(cells, flats)]
            if top is None:
                top = m
            z = z + jnp.exp(m - top)
        knocked = [jnp.sum(jnp.where(c == -jnp.inf, 1.0, 0.0), axis=0, keepdims=True) - float(n)
                   for c, n in zip(cells, n_invalid)]
        counts = [knocked[0] + knocked[1]] + knocked[2:]

        cnt = jnp.zeros((PEER_N_KEYS, td), F32)
        for a in range(PEER_TOPK):
            cnt = jnp.where(rank1 == float(a), counts[a], cnt)
        cnt_ref[0, h] = cnt
        e1_ref[0, h] = jnp.exp(s1 - v1[0])
        r2_ref[0, h] = rank2.astype(BF16)
        e2_ref[0, h] = (jnp.exp(s2 - v2[0]) / z).astype(BF16)
        return carry

    lax.fori_loop(0, PEER_HEADS, head, 0)


def _peer_topk(sc, *, tile):
    b, _, _, s = sc.shape
    spec = pl.BlockSpec((1, PEER_HEADS, PEER_N_KEYS, tile), lambda bi, i: (bi, 0, 0, i))
    shp = lambda dt: jax.ShapeDtypeStruct((b, PEER_HEADS, PEER_N_KEYS, s), dt)
    return pl.pallas_call(
        functools.partial(_peer_topk_kernel, td=tile),
        grid=(b, s // tile),
        in_specs=[pl.BlockSpec((1, 2 * PEER_HEADS, PEER_N_KEYS, tile), lambda bi, i: (bi, 0, 0, i))],
        out_specs=[spec, spec, spec, spec],
        out_shape=[shp(F32), shp(F32), shp(BF16), shp(BF16)],
        compiler_params=pltpu.CompilerParams(
            dimension_semantics=("parallel", "parallel"), vmem_limit_bytes=V7X_VMEM_LIMIT),
        name="peer_topk",
    )(sc)


def _gelu_exact(x):
    return 0.5 * x * (1.0 + lax.erf(x * (1.0 / math.sqrt(2.0))))


def _peer_ffn_kernel(xn_ref, u_ref, vt_ref, cnt_ref, e1_ref, r2_ref, e2_ref, x1_ref, o_ref,
                     acc_ref, gh_ref, *, rows_per_step):
    j = pl.program_id(2)

    @pl.when(j == 0)
    def _():
        acc_ref[...] = jnp.zeros_like(acc_ref)

    ht = _nt_dot(u_ref[...], xn_ref[0])
    act = _gelu_exact(ht).astype(BF16)
    tt = act.shape[1]
    zero = jnp.zeros((PEER_N_KEYS, tt), BF16)
    for r in range(rows_per_step):
        i1 = j * rows_per_step + r
        gate = zero
        for h in range(PEER_HEADS):
            c_row = cnt_ref[0, h, pl.ds(i1, 1), :].astype(BF16)
            e_row = e1_ref[0, h, pl.ds(i1, 1), :].astype(BF16)
            gate = gate + jnp.where(r2_ref[0, h] < c_row, e2_ref[0, h], zero) * e_row
        gh_ref[r * PEER_N_KEYS:(r + 1) * PEER_N_KEYS] = gate * act[r * PEER_N_KEYS:(r + 1) * PEER_N_KEYS]
    acc_ref[...] += _dot(vt_ref[...], gh_ref[...])

    @pl.when(j == pl.num_programs(2) - 1)
    def _():
        o_ref[0] = x1_ref[0] + acc_ref[...].T


def _peer_ffn(xn, x1, tables, u_bf, vt_bf, *, tile, experts_per_step):
    b, s, d = x1.shape
    cnt, e1, r2, e2 = tables
    rows = experts_per_step // PEER_N_KEYS
    tok = pl.BlockSpec((1, tile, d), lambda bi, i, j: (bi, i, 0))
    tab = pl.BlockSpec((1, PEER_HEADS, PEER_N_KEYS, tile), lambda bi, i, j: (bi, 0, 0, i))
    return pl.pallas_call(
        functools.partial(_peer_ffn_kernel, rows_per_step=rows),
        grid=(b, s // tile, PEER_N_EXPERTS // experts_per_step),
        in_specs=[tok,
                  pl.BlockSpec((experts_per_step, d), lambda bi, i, j: (j, 0)),
                  pl.BlockSpec((d, experts_per_step), lambda bi, i, j: (0, j)),
                  tab, tab, tab, tab, tok],
        out_specs=tok,
        out_shape=jax.ShapeDtypeStruct((b, s, d), F32),
        scratch_shapes=[pltpu.VMEM((d, tile), F32), pltpu.VMEM((experts_per_step, tile), BF16)],
        compiler_params=pltpu.CompilerParams(
            dimension_semantics=("parallel", "parallel", "arbitrary"), vmem_limit_bytes=V7X_VMEM_LIMIT),
        name="peer_ffn",
    )(xn, u_bf, vt_bf, cnt, e1, r2, e2, x1)


def _col(v, reps=1, scale=1.0):
    return (jnp.tile(v.astype(F32), reps) * scale).reshape(-1, 1)


def _layer_params(l, norm_mix_g, w_in, diff_q_norm_g, diff_k_norm_g, lam_q1, lam_k1, lam_q2, lam_k2,
                  diff_subln_g, mla_q_latent_g, mla_w_uq, mla_kv_latent_g, mla_w_ukv, mla_q_norm_g,
                  mla_k_norm_g, w_out, norm_ffn_g, peer_w_q, peer_key1, peer_key2, peer_u, peer_v):
    lam_init = 0.8 - 0.6 * math.exp(-0.3 * l)
    lam = (jnp.exp(jnp.sum(lam_q1.astype(F32) * lam_k1.astype(F32)))
           - jnp.exp(jnp.sum(lam_q2.astype(F32) * lam_k2.astype(F32))) + lam_init)
    slopes = 2.0 ** (-8.0 * jnp.arange(1, N_DIFF_HEADS + 1, dtype=F32) / N_DIFF_HEADS)
    scal = jnp.concatenate([jnp.stack([lam, jnp.asarray(1.0 - lam_init, F32)]), slopes]).astype(F32)
    keys = jnp.stack([peer_key1, peer_key2], axis=1).reshape(2 * PEER_HEADS, PEER_N_KEYS, PEER_HALF)
    return dict(
        gmix=norm_mix_g.reshape(1, -1), w_int=w_in.T.astype(BF16),
        gq=_col(diff_q_norm_g, 2 * N_DIFF_HEADS, DIFF_QK_DIM ** -0.5), gk=_col(diff_k_norm_g, 2 * N_DIFF_HEADS),
        gcq=_col(mla_q_latent_g), w_uqt=mla_w_uq.T.astype(BF16),
        gckv=_col(mla_kv_latent_g), w_ukvt=mla_w_ukv.T.astype(BF16),
        gmq=_col(mla_q_norm_g, 1, MLA_QK_DIM ** -0.5), gmk=_col(mla_k_norm_g),
        scal=scal, gsub=_col(diff_subln_g),
        w_out=w_out.astype(BF16), gffn=norm_ffn_g.reshape(1, -1), w_qt=peer_w_q.T.astype(BF16),
        keys=keys.astype(BF16), u=peer_u.astype(BF16), vt=peer_v.T.astype(BF16),
    )


def _rope_tables(s):
    half = MLA_ROPE_DIM // 2
    inv = ROPE_THETA ** (-jnp.arange(half, dtype=F32) / half)
    ang = inv[:, None] * jnp.arange(s, dtype=F32)[None, :]
    return jnp.cos(ang), jnp.sin(ang)


def _tiles(s):
    return dict(proj=min(512, s), tq=min(512, s), tk=min(512, s), topk=min(256, s), ffn=min(512, s))


def _layer(x, p):
    s = x.shape[1]
    t = _tiles(s)
    cos_t, sin_t = _rope_tables(s)
    qd, kd, vd, qm, km, vm = _pre_attn(x, p, cos_t, sin_t, tile=t["proj"])
    od = _diff_attn(p["scal"], qd, kd, vd, p["gsub"], tq=t["tq"], tk=t["tk"])
    om = _mla_attn(qm, km, vm, tq=t["tq"], tk=t["tk"])
    x1, xn, sc = _post_attn(x, od, om, p, tile=t["proj"])
    tables = _peer_topk(sc, tile=t["topk"])
    return _peer_ffn(xn, x1, tables, p["u"], p["vt"], tile=t["ffn"], experts_per_step=1024)


def kernel(x_prompt, x_sample, norm_mix_g, w_in, diff_q_norm_g, diff_k_norm_g, lam_q1, lam_k1, lam_q2, lam_k2,
           diff_subln_g, mla_q_latent_g, mla_w_uq, mla_kv_latent_g, mla_w_ukv, mla_q_norm_g, mla_k_norm_g,
           w_out, norm_ffn_g, peer_w_q, peer_key1, peer_key2, peer_u, peer_v):
    stacked = (norm_mix_g, w_in, diff_q_norm_g, diff_k_norm_g, lam_q1, lam_k1, lam_q2, lam_k2, diff_subln_g,
               mla_q_latent_g, mla_w_uq, mla_kv_latent_g, mla_w_ukv, mla_q_norm_g, mla_k_norm_g, w_out,
               norm_ffn_g, peer_w_q, peer_key1, peer_key2, peer_u, peer_v)
    y_prompt, y_sample = x_prompt, x_sample
    for l in range(DEPTH):
        p = _layer_params(l, *(w[l] for w in stacked))
        y_prompt = _layer(y_prompt, p)
        y_sample = _layer(y_sample, p)
    return (y_prompt, y_sample)
```

```python
import functools
import math

import jax
import jax.numpy as jnp
from jax import lax
from jax.experimental import pallas as pl
from jax.experimental.pallas import tpu as pltpu

F32 = jnp.float32
BF16 = jnp.bfloat16

D_MODEL = 1024
DEPTH = 2
N_DIFF_HEADS = 4
DIFF_QK_DIM = 64
DIFF_V_DIM = 128
N_MLA_HEADS = 4
MLA_Q_RANK = 256
MLA_KV_RANK = 256
MLA_NOPE_DIM = 64
MLA_ROPE_DIM = 32
MLA_QK_DIM = MLA_NOPE_DIM + MLA_ROPE_DIM
MLA_V_DIM = 128
ROPE_THETA = 10000.0
DIFF_Q_W = N_DIFF_HEADS * 2 * DIFF_QK_DIM
DIFF_V_W = N_DIFF_HEADS * DIFF_V_DIM
MLA_V_W = N_MLA_HEADS * MLA_V_DIM
IN_WIDTH = 3 * DIFF_Q_W + MLA_Q_RANK + MLA_KV_RANK + MLA_ROPE_DIM
PEER_HEADS = 8
PEER_N_KEYS = 128
PEER_N_EXPERTS = PEER_N_KEYS * PEER_N_KEYS
PEER_HALF = 128
PEER_TOPK = 16
EPS = 1e-6

LOG2E = 1.0 / math.log(2.0)
SCORE_BOUND_LOG2 = 60.0
BF16_ROUNDING_MARGIN = 1.02

HEAD_LANES = 128
V7X_VMEM_LIMIT = 56 * 1024 * 1024

_OFF_DQ = 0
_OFF_DK = DIFF_Q_W
_OFF_DV = 2 * DIFF_Q_W
_OFF_CQ = 3 * DIFF_Q_W
_OFF_CKV = _OFF_CQ + MLA_Q_RANK
_OFF_KR = _OFF_CKV + MLA_KV_RANK


def _nt_dot(a, b):
    return lax.dot_general(a, b, (((1,), (1,)), ((), ())), preferred_element_type=F32)


def _dot(a, b):
    return jnp.dot(a, b, preferred_element_type=F32)


def _group_rmsnorm_fm(a, n_groups, width, g_col):
    t = a.shape[1]
    a3 = a.reshape(n_groups, width, t)
    ms = jnp.mean(a3 * a3, axis=1, keepdims=True)
    return (a3 * lax.rsqrt(ms + EPS)).reshape(n_groups * width, t) * g_col


def _rope_fm(r, cos, sin):
    half = MLA_ROPE_DIM // 2
    r1, r2 = r[:half], r[half:]
    return r1 * cos - r2 * sin, r2 * cos + r1 * sin


def _pre_attn_kernel(x_ref, gmix_ref, wint_ref, gq_ref, gk_ref, gcq_ref, wuqt_ref, gckv_ref,
                     wukvt_ref, gmq_ref, gmk_ref, cos_ref, sin_ref,
                     qd_ref, kd_ref, vd_ref, qm_ref, km_ref, vm_ref, kt_scr):
    x = x_ref[0]
    t = x.shape[0]
    ms = jnp.mean(x * x, axis=-1, keepdims=True)
    xn = (x * lax.rsqrt(ms + EPS) * gmix_ref[...]).astype(BF16)
    ht = _nt_dot(wint_ref[...], xn)
    cos = cos_ref[...]
    sin = sin_ref[...]

    dq = _group_rmsnorm_fm(ht[_OFF_DQ:_OFF_DQ + DIFF_Q_W], 2 * N_DIFF_HEADS, DIFF_QK_DIM, gq_ref[...])
    qd_ref[0] = dq.astype(BF16)
    dk = _group_rmsnorm_fm(ht[_OFF_DK:_OFF_DK + DIFF_Q_W], 2 * N_DIFF_HEADS, DIFF_QK_DIM, gk_ref[...])
    kd_ref[0] = dk.T.astype(BF16)
    vd_ref[0] = ht[_OFF_DV:_OFF_DV + DIFF_V_W].astype(BF16)

    cq = ht[_OFF_CQ:_OFF_CQ + MLA_Q_RANK]
    cqn = (cq * lax.rsqrt(jnp.mean(cq * cq, axis=0, keepdims=True) + EPS) * gcq_ref[...]).astype(BF16)
    mq = _dot(wuqt_ref[...], cqn)
    pad = jnp.zeros((HEAD_LANES - MLA_QK_DIM, t), BF16)
    for h in range(N_MLA_HEADS):
        m = mq[h * MLA_QK_DIM:(h + 1) * MLA_QK_DIM]
        mn = m * lax.rsqrt(jnp.mean(m * m, axis=0, keepdims=True) + EPS) * gmq_ref[...]
        o1, o2 = _rope_fm(mn[MLA_NOPE_DIM:], cos, sin)
        base = h * HEAD_LANES
        qm_ref[0, base:base + MLA_NOPE_DIM] = mn[:MLA_NOPE_DIM].astype(BF16)
        qm_ref[0, base + MLA_NOPE_DIM:base + MLA_NOPE_DIM + 16] = o1.astype(BF16)
        qm_ref[0, base + MLA_NOPE_DIM + 16:base + MLA_QK_DIM] = o2.astype(BF16)
        qm_ref[0, base + MLA_QK_DIM:base + HEAD_LANES] = pad

    ckv = ht[_OFF_CKV:_OFF_CKV + MLA_KV_RANK]
    ckvn = (ckv * lax.rsqrt(jnp.mean(ckv * ckv, axis=0, keepdims=True) + EPS) * gckv_ref[...]).astype(BF16)
    kv = _dot(wukvt_ref[...], ckvn)
    kr = ht[_OFF_KR:_OFF_KR + MLA_ROPE_DIM]
    kr_ss = jnp.sum(kr * kr, axis=0, keepdims=True)
    gmk = gmk_ref[...]
    per_head = MLA_NOPE_DIM + MLA_V_DIM
    for h in range(N_MLA_HEADS):
        kn = kv[h * per_head:h * per_head + MLA_NOPE_DIM]
        v = kv[h * per_head + MLA_NOPE_DIM:(h + 1) * per_head]
        ms_k = (jnp.sum(kn * kn, axis=0, keepdims=True) + kr_ss) * (1.0 / MLA_QK_DIM)
        inv = lax.rsqrt(ms_k + EPS)
        o1, o2 = _rope_fm(kr * inv * gmk[MLA_NOPE_DIM:], cos, sin)
        base = h * HEAD_LANES
        kt_scr[base:base + MLA_NOPE_DIM] = kn * inv * gmk[:MLA_NOPE_DIM]
        kt_scr[base + MLA_NOPE_DIM:base + MLA_NOPE_DIM + 16] = o1
        kt_scr[base + MLA_NOPE_DIM + 16:base + MLA_QK_DIM] = o2
        kt_scr[base + MLA_QK_DIM:base + HEAD_LANES] = jnp.zeros((HEAD_LANES - MLA_QK_DIM, t), F32)
        vm_ref[0, base:base + HEAD_LANES] = v.astype(BF16)
    km_ref[0] = kt_scr[...].T.astype(BF16)


def _pre_attn(x, p, cos_t, sin_t, *, tile):
    b, s, d = x.shape
    hw = N_DIFF_HEADS * HEAD_LANES
    full = lambda a: pl.BlockSpec(a.shape, lambda bi, i: (0,) * a.ndim)
    fm_spec = pl.BlockSpec((1, hw, tile), lambda bi, i: (bi, 0, i))
    tm_spec = pl.BlockSpec((1, tile, hw), lambda bi, i: (bi, i, 0))
    weights = (p["gmix"], p["w_int"], p["gq"], p["gk"], p["gcq"], p["w_uqt"], p["gckv"], p["w_ukvt"],
               p["gmq"], p["gmk"])
    rope_spec = pl.BlockSpec((MLA_ROPE_DIM // 2, tile), lambda bi, i: (0, i))
    fm = jax.ShapeDtypeStruct((b, hw, s), BF16)
    tm = jax.ShapeDtypeStruct((b, s, hw), BF16)
    return pl.pallas_call(
        _pre_attn_kernel,
        grid=(b, s // tile),
        in_specs=[pl.BlockSpec((1, tile, d), lambda bi, i: (bi, i, 0))] + [full(w) for w in weights]
        + [rope_spec, rope_spec],
        out_specs=[fm_spec, tm_spec, fm_spec, fm_spec, tm_spec, fm_spec],
        out_shape=[fm, tm, fm, fm, tm, fm],
        scratch_shapes=[pltpu.VMEM((hw, tile), F32)],
        compiler_params=pltpu.CompilerParams(
            dimension_semantics=("parallel", "parallel"), vmem_limit_bytes=V7X_VMEM_LIMIT),
        name="pre_attn",
    )(x, *weights, cos_t, sin_t)


def _online_softmax_step(k, vt, qp, bias, m, l, acc_ref):
    s = _dot(k, qp)
    if bias is not None:
        s = s - bias
    mn = jnp.maximum(m, jnp.max(s, axis=0, keepdims=True))
    alpha = jnp.exp2(m - mn)
    p = jnp.exp2(s - mn)
    l = alpha * l + jnp.sum(p, axis=0, keepdims=True)
    acc_ref[...] = alpha * acc_ref[...] + _dot(vt, p.astype(BF16))
    return mn, l


def _bounded_softmax_step(k, vt, qp, bias, l, acc_ref):
    s = _dot(k, qp)
    if bias is not None:
        s = s - bias
    p = jnp.exp2(s)
    acc_ref[...] += _dot(vt, p.astype(BF16))
    return l + jnp.sum(p, axis=0, keepdims=True)


def _kv_chunk(k_ref, vt_ref, j, tk):
    off = pl.multiple_of(j * tk, tk)
    return k_ref[0, pl.ds(off, tk), :], vt_ref[0, :, pl.ds(off, tk)]


def _diff_attn_kernel(scal_ref, qt_ref, k_ref, vt_ref, g_ref, o_ref, acc0_ref, acc1_ref, *, tq, tk, n_kv):
    h = pl.program_id(1)
    i = pl.program_id(2)
    lam = scal_ref[0]
    sub_scale = scal_ref[1]
    bounded = scal_ref[2] > 0.5
    slope = scal_ref[3 + h]
    q = qt_ref[0]
    row = lax.broadcasted_iota(jnp.int32, q.shape, 0)
    zero = jnp.zeros_like(q)
    q0 = jnp.where(row < DIFF_QK_DIM, q, zero)
    q1 = jnp.where(row >= DIFF_QK_DIM, q, zero)
    d0 = (i * tq + lax.broadcasted_iota(jnp.int32, (tk, tq), 1)
          - lax.broadcasted_iota(jnp.int32, (tk, tq), 0)).astype(F32)
    acc0_ref[...] = jnp.zeros_like(acc0_ref)
    acc1_ref[...] = jnp.zeros_like(acc1_ref)

    neg = jnp.full((1, tq), -jnp.inf, F32)
    zer = jnp.zeros((1, tq), F32)

    def finish(l0, l1):
        o = acc0_ref[...] / l0 - lam * (acc1_ref[...] / l1)
        o = o * lax.rsqrt(jnp.mean(o * o, axis=0, keepdims=True) + EPS) * g_ref[...] * sub_scale
        o_ref[0] = o.T.astype(BF16)

    @pl.when(bounded)
    def _():
        def body(j, carry):
            l0, l1 = carry
            k, vt = _kv_chunk(k_ref, vt_ref, j, tk)
            bias = slope * jnp.abs(d0 - (j * tk).astype(F32))
            l0 = _bounded_softmax_step(k, vt, q0, bias, l0, acc0_ref)
            l1 = _bounded_softmax_step(k, vt, q1, bias, l1, acc1_ref)
            return l0, l1

        finish(*lax.fori_loop(0, n_kv, body, (zer, zer)))

    @pl.when(jnp.logical_not(bounded))
    def _():
        def body(j, carry):
            m0, l0, m1, l1 = carry
            k, vt = _kv_chunk(k_ref, vt_ref, j, tk)
            bias = slope * jnp.abs(d0 - (j * tk).astype(F32))
            m0, l0 = _online_softmax_step(k, vt, q0, bias, m0, l0, acc0_ref)
            m1, l1 = _online_softmax_step(k, vt, q1, bias, m1, l1, acc1_ref)
            return m0, l0, m1, l1

        _, l0, _, l1 = lax.fori_loop(0, n_kv, body, (neg, zer, neg, zer))
        finish(l0, l1)


def _mla_attn_kernel(scal_ref, qt_ref, k_ref, vt_ref, o_ref, acc_ref, *, tq, tk, n_kv):
    bounded = scal_ref[0] > 0.5
    q = qt_ref[0]
    acc_ref[...] = jnp.zeros_like(acc_ref)
    zer = jnp.zeros((1, tq), F32)

    def finish(l):
        o_ref[0] = (acc_ref[...] / l).T.astype(BF16)

    @pl.when(bounded)
    def _():
        def body(j, l):
            k, vt = _kv_chunk(k_ref, vt_ref, j, tk)
            return _bounded_softmax_step(k, vt, q, None, l, acc_ref)

        finish(lax.fori_loop(0, n_kv, body, zer))

    @pl.when(jnp.logical_not(bounded))
    def _():
        def body(j, carry):
            k, vt = _kv_chunk(k_ref, vt_ref, j, tk)
            return _online_softmax_step(k, vt, q, None, *carry, acc_ref)

        _, l = lax.fori_loop(0, n_kv, body, (jnp.full((1, tq), -jnp.inf, F32), zer))
        finish(l)


def _attn_specs(s, tq):
    q_spec = pl.BlockSpec((1, HEAD_LANES, tq), lambda bi, h, i: (bi, h, i))
    k_spec = pl.BlockSpec((1, s, HEAD_LANES), lambda bi, h, i: (bi, 0, h))
    v_spec = pl.BlockSpec((1, HEAD_LANES, s), lambda bi, h, i: (bi, h, 0))
    o_spec = pl.BlockSpec((1, tq, HEAD_LANES), lambda bi, h, i: (bi, i, h))
    return q_spec, k_spec, v_spec, o_spec


def _diff_attn(scal, qt, k, vt, g_col, *, tq, tk):
    b, hw, s = qt.shape
    q_spec, k_spec, v_spec, o_spec = _attn_specs(s, tq)
    return pl.pallas_call(
        functools.partial(_diff_attn_kernel, tq=tq, tk=tk, n_kv=s // tk),
        grid=(b, N_DIFF_HEADS, s // tq),
        in_specs=[pl.BlockSpec(memory_space=pltpu.SMEM), q_spec, k_spec, v_spec,
                  pl.BlockSpec(g_col.shape, lambda bi, h, i: (0, 0))],
        out_specs=o_spec,
        out_shape=jax.ShapeDtypeStruct((b, s, hw), BF16),
        scratch_shapes=[pltpu.VMEM((HEAD_LANES, tq), F32), pltpu.VMEM((HEAD_LANES, tq), F32)],
        compiler_params=pltpu.CompilerParams(
            dimension_semantics=("parallel", "parallel", "parallel"), vmem_limit_bytes=V7X_VMEM_LIMIT),
        name="diff_attn",
    )(scal, qt, k, vt, g_col)


def _mla_attn(scal, qt, k, vt, *, tq, tk):
    b, hw, s = qt.shape
    q_spec, k_spec, v_spec, o_spec = _attn_specs(s, tq)
    return pl.pallas_call(
        functools.partial(_mla_attn_kernel, tq=tq, tk=tk, n_kv=s // tk),
        grid=(b, N_MLA_HEADS, s // tq),
        in_specs=[pl.BlockSpec(memory_space=pltpu.SMEM), q_spec, k_spec, v_spec],
        out_specs=o_spec,
        out_shape=jax.ShapeDtypeStruct((b, s, hw), BF16),
        scratch_shapes=[pltpu.VMEM((HEAD_LANES, tq), F32)],
        compiler_params=pltpu.CompilerParams(
            dimension_semantics=("parallel", "parallel", "parallel"), vmem_limit_bytes=V7X_VMEM_LIMIT),
        name="mla_attn",
    )(scal, qt, k, vt)


def _post_attn_kernel(x_ref, od_ref, om_ref, wout_ref, gffn_ref, wqt_ref, keys_ref,
                      x1_ref, xn_ref, sc_ref):
    x1 = (x_ref[0] + _dot(od_ref[0], wout_ref[:DIFF_V_W]) + _dot(om_ref[0], wout_ref[DIFF_V_W:]))
    x1_ref[0] = x1
    ms = jnp.mean(x1 * x1, axis=-1, keepdims=True)
    xn = (x1 * lax.rsqrt(ms + EPS) * gffn_ref[...]).astype(BF16)
    xn_ref[0] = xn
    qt = _nt_dot(wqt_ref[...], xn).astype(BF16)
    for hs in range(2 * PEER_HEADS):
        sc_ref[0, hs] = _dot(keys_ref[hs], qt[hs * PEER_HALF:(hs + 1) * PEER_HALF])


def _post_attn(x, od, om, p, *, tile):
    b, s, d = x.shape
    full = lambda a: pl.BlockSpec(a.shape, lambda bi, i: (0,) * a.ndim)
    tok = lambda w: pl.BlockSpec((1, tile, w), lambda bi, i: (bi, i, 0))
    weights = (p["w_out"], p["gffn"], p["w_qt"], p["keys"])
    return pl.pallas_call(
        _post_attn_kernel,
        grid=(b, s // tile),
        in_specs=[tok(d), tok(DIFF_V_W), tok(MLA_V_W)] + [full(w) for w in weights],
        out_specs=[tok(d), tok(d),
                   pl.BlockSpec((1, 2 * PEER_HEADS, PEER_N_KEYS, tile), lambda bi, i: (bi, 0, 0, i))],
        out_shape=[jax.ShapeDtypeStruct((b, s, d), F32), jax.ShapeDtypeStruct((b, s, d), BF16),
                   jax.ShapeDtypeStruct((b, 2 * PEER_HEADS, PEER_N_KEYS, s), F32)],
        compiler_params=pltpu.CompilerParams(
            dimension_semantics=("parallel", "parallel"), vmem_limit_bytes=V7X_VMEM_LIMIT),
        name="post_attn",
    )(x, od, om, *weights)


_SUB = 8
LANES = 128
BF16_ROWS = 16


def _top16_ranked(s, sub_idx, row16):
    rank = jnp.full(s.shape, float(PEER_N_KEYS - 1), F32)
    stacked = jnp.zeros(row16.shape, F32)
    vals = []
    for k in range(PEER_TOPK):
        m = jnp.max(s, axis=0, keepdims=True)
        first = jnp.min(jnp.where(s == m, sub_idx, float(PEER_N_KEYS)), axis=0, keepdims=True)
        sel = sub_idx == first
        rank = jnp.where(sel, float(k), rank)
        s = jnp.where(sel, -jnp.inf, s)
        stacked = jnp.where(row16 == float(k), m, stacked)
        vals.append(m)
    return vals, stacked, rank


_CODE_BASE = -(2.0 ** 126)
_CODE_STEP = 2.0 ** 121


def _top16_untied(s, row16):
    stacked = jnp.zeros(row16.shape, F32)
    vals = []
    for k in range(PEER_TOPK):
        m = jnp.max(s, axis=0, keepdims=True)
        s = jnp.where(s == m, _CODE_BASE - k * _CODE_STEP, s)
        stacked = jnp.where(row16 == float(k), m, stacked)
        vals.append(m)
    coded = s <= _CODE_BASE
    rank = jnp.where(coded, s * (-1.0 / _CODE_STEP) - 32.0, float(PEER_N_KEYS - 1))
    n_coded = jnp.sum(jnp.where(coded, 1.0, 0.0), axis=0, keepdims=True)
    return vals, stacked, rank, n_coded


def _dup_bf16(v):
    bits = lax.bitcast_convert_type(v.astype(BF16).astype(F32), jnp.uint32)
    return bits | (bits >> 16)


def _peer_topk_kernel(sc_ref, cnt_ref, e1_ref, r2_ref, e2_ref, *, td):
    sub_idx = lax.broadcasted_iota(jnp.int32, (PEER_N_KEYS, td), 0).astype(F32)
    row16 = lax.broadcasted_iota(jnp.int32, (PEER_TOPK, td), 0).astype(F32)
    row8 = lax.broadcasted_iota(jnp.int32, (_SUB, td), 0).astype(F32)
    neg_inf = jnp.full((_SUB, td), -jnp.inf, F32)
    k_top = float(PEER_TOPK)

    def make_head(exact_ties):
        def head(h, bad):
            s1 = sc_ref[0, 2 * h]
            s2 = sc_ref[0, 2 * h + 1]
            if exact_ties:
                v1, _, rank1 = _top16_ranked(s1, sub_idx, row16)
                v2, v2s, rank2 = _top16_ranked(s2, sub_idx, row16)
            else:
                v1, _, rank1, n1 = _top16_untied(s1, row16)
                v2, v2s, rank2, n2 = _top16_untied(s2, row16)
                bad = jnp.maximum(bad, jnp.abs(n1 - k_top) + jnp.abs(n2 - k_top))

            cells, flats, n_invalid = [], [], []
            cells.append(v1[0] + v2s[:_SUB]); flats.append(row8); n_invalid.append(0)
            cells.append(v1[0] + v2s[_SUB:]); flats.append(row8 + float(_SUB)); n_invalid.append(0)
            for a in range(1, PEER_TOPK):
                nb = PEER_TOPK // (a + 1)
                cells.append(jnp.where(row8 < float(nb), v1[a] + v2s[:_SUB], neg_inf))
                flats.append(row8 + float(a * PEER_TOPK))
                n_invalid.append(_SUB - nb)
            big = float(PEER_TOPK * PEER_TOPK)
            top = None
            z = jnp.zeros((1, td), F32)
            for _ in range(PEER_TOPK):
                m = jnp.max(functools.reduce(jnp.maximum, cells), axis=0, keepdims=True)
                if exact_ties:
                    cand = [jnp.where(c == m, f, big) for c, f in zip(cells, flats)]
                    first = jnp.min(functools.reduce(jnp.minimum, cand), axis=0, keepdims=True)
                    cells = [jnp.where(f == first, -jnp.inf, c) for c, f in zip(cells, flats)]
                else:
                    cells = [jnp.where(c == m, -jnp.inf, c) for c in cells]
                if top is None:
                    top = m
                z = z + jnp.exp(m - top)
            knocked = [jnp.sum(jnp.where(c == -jnp.inf, 1.0, 0.0), axis=0, keepdims=True) - float(n)
                       for c, n in zip(cells, n_invalid)]
            counts = [knocked[0] + knocked[1]] + knocked[2:]
            if not exact_ties:
                bad = jnp.maximum(bad, jnp.abs(functools.reduce(jnp.add, counts) - k_top))

            cnt = jnp.zeros((PEER_N_KEYS, td), F32)
            for a in range(PEER_TOPK):
                cnt = jnp.where(rank1 == float(a), counts[a], cnt)
            e1 = jnp.exp(s1 - v1[0])
            e2 = jnp.exp(s2 - v2[0]) * (1.0 / z)
            for lt in range(td // LANES):
                tok = slice(lt * LANES, (lt + 1) * LANES)
                cnt_ref[0, lt, h] = _dup_bf16(cnt[:, tok])
                e1_ref[0, lt, h] = _dup_bf16(e1[:, tok])
            r2_ref[0, h] = rank2.astype(BF16)
            e2_ref[0, h] = e2.astype(BF16)
            return bad

        return head

    no_bad = jnp.zeros((1, td), F32)
    bad = lax.fori_loop(0, PEER_HEADS, make_head(False), no_bad)

    @pl.when(jnp.max(bad) > 0.0)
    def _():
        lax.fori_loop(0, PEER_HEADS, make_head(True), no_bad)


def _peer_topk(sc, *, tile):
    b, _, _, s = sc.shape
    row_spec = pl.BlockSpec((1, tile // LANES, PEER_HEADS, PEER_N_KEYS, LANES), lambda bi, i: (bi, i, 0, 0, 0))
    row_shape = jax.ShapeDtypeStruct((b, s // LANES, PEER_HEADS, PEER_N_KEYS, LANES), jnp.uint32)
    key_spec = pl.BlockSpec((1, PEER_HEADS, PEER_N_KEYS, tile), lambda bi, i: (bi, 0, 0, i))
    key_shape = jax.ShapeDtypeStruct((b, PEER_HEADS, PEER_N_KEYS, s), BF16)
    return pl.pallas_call(
        functools.partial(_peer_topk_kernel, td=tile),
        grid=(b, s // tile),
        in_specs=[pl.BlockSpec((1, 2 * PEER_HEADS, PEER_N_KEYS, tile), lambda bi, i: (bi, 0, 0, i))],
        out_specs=[row_spec, row_spec, key_spec, key_spec],
        out_shape=[row_shape, row_shape, key_shape, key_shape],
        compiler_params=pltpu.CompilerParams(
            dimension_semantics=("parallel", "parallel"), vmem_limit_bytes=V7X_VMEM_LIMIT),
        name="peer_topk",
    )(sc)


def _gelu_exact(x):
    return 0.5 * x * (1.0 + lax.erf(x * (1.0 / math.sqrt(2.0))))


def _packed_row(ref, lane_tiles, h, i1):
    def one(lt):
        row = jnp.broadcast_to(ref[0, lt, h, pl.ds(i1, 1), :], (PEER_N_KEYS // 2, LANES))
        return pltpu.bitcast(row, BF16)
    return jnp.concatenate([one(lt) for lt in lane_tiles], axis=1)


def _peer_ffn_kernel(xn_ref, u_ref, vt_ref, cnt_ref, e1_ref, r2_ref, e2_ref, x1_ref, o_ref,
                     acc_ref, gh_ref, *, rows_per_step, chunk):
    j = pl.program_id(2)

    @pl.when(j == 0)
    def _():
        acc_ref[...] = jnp.zeros_like(acc_ref)

    u = u_ref[...]
    vt = vt_ref[...]
    zero = jnp.zeros((PEER_N_KEYS, chunk), BF16)
    lanes_per_chunk = chunk // LANES
    for c in range(xn_ref.shape[1] // chunk):
        tok = slice(c * chunk, (c + 1) * chunk)
        lane_tiles = range(c * lanes_per_chunk, (c + 1) * lanes_per_chunk)
        ht = _nt_dot(u, xn_ref[0, tok, :])
        act = _gelu_exact(ht).astype(BF16)
        for r in range(rows_per_step):
            i1 = j * rows_per_step + r
            keys = slice(r * PEER_N_KEYS, (r + 1) * PEER_N_KEYS)
            gate = zero
            for h in range(PEER_HEADS):
                count = _packed_row(cnt_ref, lane_tiles, h, i1)
                e1 = _packed_row(e1_ref, lane_tiles, h, i1)
                gate = gate + jnp.where(r2_ref[0, h, :, tok] < count, e2_ref[0, h, :, tok], zero) * e1
            gh_ref[c, keys, :] = gate * act[keys]
        acc_ref[:, tok] += _dot(vt, gh_ref[c])

    @pl.when(j == pl.num_programs(2) - 1)
    def _():
        o_ref[0] = x1_ref[0] + acc_ref[...].T


def _peer_ffn(xn, x1, tables, u_bf, vt_bf, *, tile, chunk, experts_per_step):
    b, s, d = x1.shape
    cnt, e1, r2, e2 = tables
    rows = experts_per_step // PEER_N_KEYS
    tok = pl.BlockSpec((1, tile, d), lambda bi, i, j: (bi, i, 0))
    row_tab = pl.BlockSpec((1, tile // LANES, PEER_HEADS, PEER_N_KEYS, LANES), lambda bi, i, j: (bi, i, 0, 0, 0))
    key_tab = pl.BlockSpec((1, PEER_HEADS, PEER_N_KEYS, tile), lambda bi, i, j: (bi, 0, 0, i))
    return pl.pallas_call(
        functools.partial(_peer_ffn_kernel, rows_per_step=rows, chunk=chunk),
        grid=(b, s // tile, PEER_N_EXPERTS // experts_per_step),
        in_specs=[tok,
                  pl.BlockSpec((experts_per_step, d), lambda bi, i, j: (j, 0)),
                  pl.BlockSpec((d, experts_per_step), lambda bi, i, j: (0, j)),
                  row_tab, row_tab, key_tab, key_tab, tok],
        out_specs=tok,
        out_shape=jax.ShapeDtypeStruct((b, s, d), F32),
        scratch_shapes=[pltpu.VMEM((d, tile), F32),
                        pltpu.VMEM((tile // chunk, experts_per_step, chunk), BF16)],
        compiler_params=pltpu.CompilerParams(
            dimension_semantics=("parallel", "parallel", "arbitrary"), vmem_limit_bytes=V7X_VMEM_LIMIT),
        name="peer_ffn",
    )(xn, u_bf, vt_bf, cnt, e1, r2, e2, x1)


def _col(v, reps=1, scale=1.0):
    return (jnp.tile(v.astype(F32), reps) * scale).reshape(-1, 1)


def _score_bound(scale_dim, gq, gk):
    bound = scale_dim * jnp.max(jnp.abs(gq.astype(F32))) * jnp.max(jnp.abs(gk.astype(F32))) * BF16_ROUNDING_MARGIN
    return (bound <= SCORE_BOUND_LOG2).astype(F32)


def _layer_params(l, norm_mix_g, w_in, diff_q_norm_g, diff_k_norm_g, lam_q1, lam_k1, lam_q2, lam_k2,
                  diff_subln_g, mla_q_latent_g, mla_w_uq, mla_kv_latent_g, mla_w_ukv, mla_q_norm_g,
                  mla_k_norm_g, w_out, norm_ffn_g, peer_w_q, peer_key1, peer_key2, peer_u, peer_v):
    lam_init = 0.8 - 0.6 * math.exp(-0.3 * l)
    lam = (jnp.exp(jnp.sum(lam_q1.astype(F32) * lam_k1.astype(F32)))
           - jnp.exp(jnp.sum(lam_q2.astype(F32) * lam_k2.astype(F32))) + lam_init)
    slopes = LOG2E * 2.0 ** (-8.0 * jnp.arange(1, N_DIFF_HEADS + 1, dtype=F32) / N_DIFF_HEADS)
    diff_scale = LOG2E * DIFF_QK_DIM ** -0.5
    mla_scale = LOG2E * MLA_QK_DIM ** -0.5
    diff_bound = _score_bound(diff_scale * DIFF_QK_DIM, diff_q_norm_g, diff_k_norm_g)
    mla_bound = _score_bound(mla_scale * MLA_QK_DIM, mla_q_norm_g, mla_k_norm_g)
    scal = jnp.concatenate([jnp.stack([lam, jnp.asarray(1.0 - lam_init, F32), diff_bound]), slopes]).astype(F32)
    keys = jnp.stack([peer_key1, peer_key2], axis=1).reshape(2 * PEER_HEADS, PEER_N_KEYS, PEER_HALF)
    return dict(
        gmix=norm_mix_g.reshape(1, -1), w_int=w_in.T.astype(BF16),
        gq=_col(diff_q_norm_g, 2 * N_DIFF_HEADS, diff_scale), gk=_col(diff_k_norm_g, 2 * N_DIFF_HEADS),
        gcq=_col(mla_q_latent_g), w_uqt=mla_w_uq.T.astype(BF16),
        gckv=_col(mla_kv_latent_g), w_ukvt=mla_w_ukv.T.astype(BF16),
        gmq=_col(mla_q_norm_g, 1, mla_scale), gmk=_col(mla_k_norm_g),
        scal=scal, mla_scal=mla_bound.reshape(1), gsub=_col(diff_subln_g),
        w_out=w_out.astype(BF16), gffn=norm_ffn_g.reshape(1, -1), w_qt=peer_w_q.T.astype(BF16),
        keys=keys.astype(BF16), u=peer_u.astype(BF16), vt=peer_v.T.astype(BF16),
    )


def _rope_tables(s):
    half = MLA_ROPE_DIM // 2
    inv = ROPE_THETA ** (-jnp.arange(half, dtype=F32) / half)
    ang = inv[:, None] * jnp.arange(s, dtype=F32)[None, :]
    return jnp.cos(ang), jnp.sin(ang)


def _tiles(s):
    return dict(proj=min(512, s), tq=min(512, s), tk=min(512, s), topk=min(256, s), ffn=min(512, s))


def _layer(x, p):
    s = x.shape[1]
    t = _tiles(s)
    cos_t, sin_t = _rope_tables(s)
    qd, kd, vd, qm, km, vm = _pre_attn(x, p, cos_t, sin_t, tile=t["proj"])
    od = _diff_attn(p["scal"], qd, kd, vd, p["gsub"], tq=t["tq"], tk=t["tk"])
    om = _mla_attn(p["mla_scal"], qm, km, vm, tq=t["tq"], tk=t["tk"])
    x1, xn, sc = _post_attn(x, od, om, p, tile=t["proj"])
    tables = _peer_topk(sc, tile=t["topk"])
    return _peer_ffn(xn, x1, tables, p["u"], p["vt"], tile=t["ffn"], chunk=t["ffn"], experts_per_step=1024)


def kernel(x_prompt, x_sample, norm_mix_g, w_in, diff_q_norm_g, diff_k_norm_g, lam_q1, lam_k1, lam_q2, lam_k2,
           diff_subln_g, mla_q_latent_g, mla_w_uq, mla_kv_latent_g, mla_w_ukv, mla_q_norm_g, mla_k_norm_g,
           w_out, norm_ffn_g, peer_w_q, peer_key1, peer_key2, peer_u, peer_v):
    stacked = (norm_mix_g, w_in, diff_q_norm_g, diff_k_norm_g, lam_q1, lam_k1, lam_q2, lam_k2, diff_subln_g,
               mla_q_latent_g, mla_w_uq, mla_kv_latent_g, mla_w_ukv, mla_q_norm_g, mla_k_norm_g, w_out,
               norm_ffn_g, peer_w_q, peer_key1, peer_key2, peer_u, peer_v)
    y_prompt, y_sample = x_prompt, x_sample
    for l in range(DEPTH):
        p = _layer_params(l, *(w[l] for w in stacked))
        y_prompt = _layer(y_prompt, p)
        y_sample = _layer(y_sample, p)
    return (y_prompt, y_sample)
```

```python
import functools
import math

import jax
import jax.numpy as jnp
from jax import lax
from jax.experimental import pallas as pl
from jax.experimental.pallas import tpu as pltpu

F32 = jnp.float32
BF16 = jnp.bfloat16

D_MODEL = 1024
DEPTH = 2
N_DIFF_HEADS = 4
DIFF_QK_DIM = 64
DIFF_V_DIM = 128
N_MLA_HEADS = 4
MLA_Q_RANK = 256
MLA_KV_RANK = 256
MLA_NOPE_DIM = 64
MLA_ROPE_DIM = 32
MLA_QK_DIM = MLA_NOPE_DIM + MLA_ROPE_DIM
MLA_V_DIM = 128
ROPE_THETA = 10000.0
DIFF_Q_W = N_DIFF_HEADS * 2 * DIFF_QK_DIM
DIFF_V_W = N_DIFF_HEADS * DIFF_V_DIM
MLA_V_W = N_MLA_HEADS * MLA_V_DIM
IN_WIDTH = 3 * DIFF_Q_W + MLA_Q_RANK + MLA_KV_RANK + MLA_ROPE_DIM
PEER_HEADS = 8
PEER_N_KEYS = 128
PEER_N_EXPERTS = PEER_N_KEYS * PEER_N_KEYS
PEER_HALF = 128
PEER_TOPK = 16
EPS = 1e-6

LOG2E = 1.0 / math.log(2.0)
SCORE_BOUND_LOG2 = 60.0
BF16_ROUNDING_MARGIN = 1.02

KV_UNROLL = 4

HEAD_LANES = 128
V7X_VMEM_LIMIT = 56 * 1024 * 1024

_OFF_DQ = 0
_OFF_DK = DIFF_Q_W
_OFF_DV = 2 * DIFF_Q_W
_OFF_CQ = 3 * DIFF_Q_W
_OFF_CKV = _OFF_CQ + MLA_Q_RANK
_OFF_KR = _OFF_CKV + MLA_KV_RANK


def _nt_dot(a, b):
    return lax.dot_general(a, b, (((1,), (1,)), ((), ())), preferred_element_type=F32)


def _dot(a, b):
    return jnp.dot(a, b, preferred_element_type=F32)


def _group_rmsnorm_fm(a, n_groups, width, g_col):
    t = a.shape[1]
    a3 = a.reshape(n_groups, width, t)
    ms = jnp.mean(a3 * a3, axis=1, keepdims=True)
    return (a3 * lax.rsqrt(ms + EPS)).reshape(n_groups * width, t) * g_col


def _rope_fm(r, cos, sin):
    half = MLA_ROPE_DIM // 2
    r1, r2 = r[:half], r[half:]
    return r1 * cos - r2 * sin, r2 * cos + r1 * sin


def _pre_attn_kernel(x_ref, gmix_ref, wint_ref, gq_ref, gk_ref, gcq_ref, wuqt_ref, gckv_ref,
                     wukvt_ref, gmq_ref, gmk_ref, cos_ref, sin_ref,
                     qd_ref, kd_ref, vd_ref, qm_ref, km_ref, vm_ref, kt_scr):
    x = x_ref[0]
    t = x.shape[0]
    ms = jnp.mean(x * x, axis=-1, keepdims=True)
    xn = (x * lax.rsqrt(ms + EPS) * gmix_ref[...]).astype(BF16)
    ht = _nt_dot(wint_ref[...], xn)
    cos = cos_ref[...]
    sin = sin_ref[...]

    dq = _group_rmsnorm_fm(ht[_OFF_DQ:_OFF_DQ + DIFF_Q_W], 2 * N_DIFF_HEADS, DIFF_QK_DIM, gq_ref[...])
    qd_ref[0] = dq.astype(BF16)
    dk = _group_rmsnorm_fm(ht[_OFF_DK:_OFF_DK + DIFF_Q_W], 2 * N_DIFF_HEADS, DIFF_QK_DIM, gk_ref[...])
    kd_ref[0] = dk.T.astype(BF16)
    vd_ref[0] = ht[_OFF_DV:_OFF_DV + DIFF_V_W].astype(BF16)

    cq = ht[_OFF_CQ:_OFF_CQ + MLA_Q_RANK]
    cqn = (cq * lax.rsqrt(jnp.mean(cq * cq, axis=0, keepdims=True) + EPS) * gcq_ref[...]).astype(BF16)
    mq = _dot(wuqt_ref[...], cqn)
    pad = jnp.zeros((HEAD_LANES - MLA_QK_DIM, t), BF16)
    for h in range(N_MLA_HEADS):
        m = mq[h * MLA_QK_DIM:(h + 1) * MLA_QK_DIM]
        mn = m * lax.rsqrt(jnp.mean(m * m, axis=0, keepdims=True) + EPS) * gmq_ref[...]
        o1, o2 = _rope_fm(mn[MLA_NOPE_DIM:], cos, sin)
        base = h * HEAD_LANES
        qm_ref[0, base:base + MLA_NOPE_DIM] = mn[:MLA_NOPE_DIM].astype(BF16)
        qm_ref[0, base + MLA_NOPE_DIM:base + MLA_NOPE_DIM + 16] = o1.astype(BF16)
        qm_ref[0, base + MLA_NOPE_DIM + 16:base + MLA_QK_DIM] = o2.astype(BF16)
        qm_ref[0, base + MLA_QK_DIM:base + HEAD_LANES] = pad

    ckv = ht[_OFF_CKV:_OFF_CKV + MLA_KV_RANK]
    ckvn = (ckv * lax.rsqrt(jnp.mean(ckv * ckv, axis=0, keepdims=True) + EPS) * gckv_ref[...]).astype(BF16)
    kv = _dot(wukvt_ref[...], ckvn)
    kr = ht[_OFF_KR:_OFF_KR + MLA_ROPE_DIM]
    kr_ss = jnp.sum(kr * kr, axis=0, keepdims=True)
    gmk = gmk_ref[...]
    per_head = MLA_NOPE_DIM + MLA_V_DIM
    for h in range(N_MLA_HEADS):
        kn = kv[h * per_head:h * per_head + MLA_NOPE_DIM]
        v = kv[h * per_head + MLA_NOPE_DIM:(h + 1) * per_head]
        ms_k = (jnp.sum(kn * kn, axis=0, keepdims=True) + kr_ss) * (1.0 / MLA_QK_DIM)
        inv = lax.rsqrt(ms_k + EPS)
        o1, o2 = _rope_fm(kr * inv * gmk[MLA_NOPE_DIM:], cos, sin)
        base = h * HEAD_LANES
        kt_scr[base:base + MLA_NOPE_DIM] = kn * inv * gmk[:MLA_NOPE_DIM]
        kt_scr[base + MLA_NOPE_DIM:base + MLA_NOPE_DIM + 16] = o1
        kt_scr[base + MLA_NOPE_DIM + 16:base + MLA_QK_DIM] = o2
        kt_scr[base + MLA_QK_DIM:base + HEAD_LANES] = jnp.zeros((HEAD_LANES - MLA_QK_DIM, t), F32)
        vm_ref[0, base:base + HEAD_LANES] = v.astype(BF16)
    km_ref[0] = kt_scr[...].T.astype(BF16)


def _pre_attn(x, p, cos_t, sin_t, *, tile):
    b, s, d = x.shape
    hw = N_DIFF_HEADS * HEAD_LANES
    full = lambda a: pl.BlockSpec(a.shape, lambda bi, i: (0,) * a.ndim)
    fm_spec = pl.BlockSpec((1, hw, tile), lambda bi, i: (bi, 0, i))
    tm_spec = pl.BlockSpec((1, tile, hw), lambda bi, i: (bi, i, 0))
    weights = (p["gmix"], p["w_int"], p["gq"], p["gk"], p["gcq"], p["w_uqt"], p["gckv"], p["w_ukvt"],
               p["gmq"], p["gmk"])
    rope_spec = pl.BlockSpec((MLA_ROPE_DIM // 2, tile), lambda bi, i: (0, i))
    fm = jax.ShapeDtypeStruct((b, hw, s), BF16)
    tm = jax.ShapeDtypeStruct((b, s, hw), BF16)
    return pl.pallas_call(
        _pre_attn_kernel,
        grid=(b, s // tile),
        in_specs=[pl.BlockSpec((1, tile, d), lambda bi, i: (bi, i, 0))] + [full(w) for w in weights]
        + [rope_spec, rope_spec],
        out_specs=[fm_spec, tm_spec, fm_spec, fm_spec, tm_spec, fm_spec],
        out_shape=[fm, tm, fm, fm, tm, fm],
        scratch_shapes=[pltpu.VMEM((hw, tile), F32)],
        compiler_params=pltpu.CompilerParams(
            dimension_semantics=("parallel", "parallel"), vmem_limit_bytes=V7X_VMEM_LIMIT),
        name="pre_attn",
    )(x, *weights, cos_t, sin_t)


def _online_softmax_step(k, vt, qp, bias, m, l, acc_ref):
    s = _dot(k, qp)
    if bias is not None:
        s = s - bias
    mn = jnp.maximum(m, jnp.max(s, axis=0, keepdims=True))
    alpha = jnp.exp2(m - mn)
    p = jnp.exp2(s - mn)
    l = alpha * l + jnp.sum(p, axis=0, keepdims=True)
    acc_ref[...] = alpha * acc_ref[...] + _dot(vt, p.astype(BF16))
    return mn, l


def _bounded_softmax_step(k, vt, qp, bias, l, acc_ref):
    s = _dot(k, qp)
    if bias is not None:
        s = s - bias
    p = jnp.exp2(s)
    acc_ref[...] += _dot(vt, p.astype(BF16))
    return l + jnp.sum(p, axis=0, keepdims=True)


def _kv_chunk(k_ref, vt_ref, j, tk):
    off = pl.multiple_of(j * tk, tk)
    return k_ref[0, pl.ds(off, tk), :], vt_ref[0, :, pl.ds(off, tk)]


def _diff_attn_kernel(scal_ref, qt_ref, k_ref, vt_ref, g_ref, o_ref, acc0_ref, acc1_ref, *, tq, tk, n_kv):
    h = pl.program_id(1)
    i = pl.program_id(2)
    lam = scal_ref[0]
    sub_scale = scal_ref[1]
    bounded = scal_ref[2] > 0.5
    slope = scal_ref[3 + h]
    q = qt_ref[0]
    row = lax.broadcasted_iota(jnp.int32, q.shape, 0)
    zero = jnp.zeros_like(q)
    q0 = jnp.where(row < DIFF_QK_DIM, q, zero)
    q1 = jnp.where(row >= DIFF_QK_DIM, q, zero)
    d0 = (i * tq + lax.broadcasted_iota(jnp.int32, (tk, tq), 1)
          - lax.broadcasted_iota(jnp.int32, (tk, tq), 0)).astype(F32)
    acc0_ref[...] = jnp.zeros_like(acc0_ref)
    acc1_ref[...] = jnp.zeros_like(acc1_ref)

    neg = jnp.full((1, tq), -jnp.inf, F32)
    zer = jnp.zeros((1, tq), F32)

    def finish(l0, l1):
        o = acc0_ref[...] / l0 - lam * (acc1_ref[...] / l1)
        o = o * lax.rsqrt(jnp.mean(o * o, axis=0, keepdims=True) + EPS) * g_ref[...] * sub_scale
        o_ref[0] = o.T.astype(BF16)

    @pl.when(bounded)
    def _():
        def body(j, carry):
            l0, l1 = carry
            k, vt = _kv_chunk(k_ref, vt_ref, j, tk)
            bias = slope * jnp.abs(d0 - lax.convert_element_type(j * tk, F32))
            l0 = _bounded_softmax_step(k, vt, q0, bias, l0, acc0_ref)
            l1 = _bounded_softmax_step(k, vt, q1, bias, l1, acc1_ref)
            return l0, l1

        finish(*lax.fori_loop(0, n_kv, body, (zer, zer), unroll=KV_UNROLL))

    @pl.when(jnp.logical_not(bounded))
    def _():
        def body(j, carry):
            m0, l0, m1, l1 = carry
            k, vt = _kv_chunk(k_ref, vt_ref, j, tk)
            bias = slope * jnp.abs(d0 - lax.convert_element_type(j * tk, F32))
            m0, l0 = _online_softmax_step(k, vt, q0, bias, m0, l0, acc0_ref)
            m1, l1 = _online_softmax_step(k, vt, q1, bias, m1, l1, acc1_ref)
            return m0, l0, m1, l1

        _, l0, _, l1 = lax.fori_loop(0, n_kv, body, (neg, zer, neg, zer))
        finish(l0, l1)


def _mla_attn_kernel(scal_ref, qt_ref, k_ref, vt_ref, o_ref, acc_ref, *, tq, tk, n_kv):
    bounded = scal_ref[0] > 0.5
    q = qt_ref[0]
    acc_ref[...] = jnp.zeros_like(acc_ref)
    zer = jnp.zeros((1, tq), F32)

    def finish(l):
        o_ref[0] = (acc_ref[...] / l).T.astype(BF16)

    @pl.when(bounded)
    def _():
        def body(j, l):
            k, vt = _kv_chunk(k_ref, vt_ref, j, tk)
            return _bounded_softmax_step(k, vt, q, None, l, acc_ref)

        finish(lax.fori_loop(0, n_kv, body, zer, unroll=KV_UNROLL))

    @pl.when(jnp.logical_not(bounded))
    def _():
        def body(j, carry):
            k, vt = _kv_chunk(k_ref, vt_ref, j, tk)
            return _online_softmax_step(k, vt, q, None, *carry, acc_ref)

        _, l = lax.fori_loop(0, n_kv, body, (jnp.full((1, tq), -jnp.inf, F32), zer))
        finish(l)


def _attn_specs(s, tq):
    q_spec = pl.BlockSpec((1, HEAD_LANES, tq), lambda bi, h, i: (bi, h, i))
    k_spec = pl.BlockSpec((1, s, HEAD_LANES), lambda bi, h, i: (bi, 0, h))
    v_spec = pl.BlockSpec((1, HEAD_LANES, s), lambda bi, h, i: (bi, h, 0))
    o_spec = pl.BlockSpec((1, tq, HEAD_LANES), lambda bi, h, i: (bi, i, h))
    return q_spec, k_spec, v_spec, o_spec


def _diff_attn(scal, qt, k, vt, g_col, *, tq, tk):
    b, hw, s = qt.shape
    q_spec, k_spec, v_spec, o_spec = _attn_specs(s, tq)
    return pl.pallas_call(
        functools.partial(_diff_attn_kernel, tq=tq, tk=tk, n_kv=s // tk),
        grid=(b, N_DIFF_HEADS, s // tq),
        in_specs=[pl.BlockSpec(memory_space=pltpu.SMEM), q_spec, k_spec, v_spec,
                  pl.BlockSpec(g_col.shape, lambda bi, h, i: (0, 0))],
        out_specs=o_spec,
        out_shape=jax.ShapeDtypeStruct((b, s, hw), BF16),
        scratch_shapes=[pltpu.VMEM((HEAD_LANES, tq), F32), pltpu.VMEM((HEAD_LANES, tq), F32)],
        compiler_params=pltpu.CompilerParams(
            dimension_semantics=("parallel", "parallel", "parallel"), vmem_limit_bytes=V7X_VMEM_LIMIT),
        name="diff_attn",
    )(scal, qt, k, vt, g_col)


def _mla_attn(scal, qt, k, vt, *, tq, tk):
    b, hw, s = qt.shape
    q_spec, k_spec, v_spec, o_spec = _attn_specs(s, tq)
    return pl.pallas_call(
        functools.partial(_mla_attn_kernel, tq=tq, tk=tk, n_kv=s // tk),
        grid=(b, N_MLA_HEADS, s // tq),
        in_specs=[pl.BlockSpec(memory_space=pltpu.SMEM), q_spec, k_spec, v_spec],
        out_specs=o_spec,
        out_shape=jax.ShapeDtypeStruct((b, s, hw), BF16),
        scratch_shapes=[pltpu.VMEM((HEAD_LANES, tq), F32)],
        compiler_params=pltpu.CompilerParams(
            dimension_semantics=("parallel", "parallel", "parallel"), vmem_limit_bytes=V7X_VMEM_LIMIT),
        name="mla_attn",
    )(scal, qt, k, vt)


def _post_attn_kernel(x_ref, od_ref, om_ref, wout_ref, gffn_ref, wqt_ref, keys_ref,
                      x1_ref, xn_ref, sc_ref):
    x1 = (x_ref[0] + _dot(od_ref[0], wout_ref[:DIFF_V_W]) + _dot(om_ref[0], wout_ref[DIFF_V_W:]))
    x1_ref[0] = x1
    ms = jnp.mean(x1 * x1, axis=-1, keepdims=True)
    xn = (x1 * lax.rsqrt(ms + EPS) * gffn_ref[...]).astype(BF16)
    xn_ref[0] = xn
    qt = _nt_dot(wqt_ref[...], xn).astype(BF16)
    for hs in range(2 * PEER_HEADS):
        sc_ref[0, hs] = _dot(keys_ref[hs], qt[hs * PEER_HALF:(hs + 1) * PEER_HALF])


def _post_attn(x, od, om, p, *, tile):
    b, s, d = x.shape
    full = lambda a: pl.BlockSpec(a.shape, lambda bi, i: (0,) * a.ndim)
    tok = lambda w: pl.BlockSpec((1, tile, w), lambda bi, i: (bi, i, 0))
    weights = (p["w_out"], p["gffn"], p["w_qt"], p["keys"])
    return pl.pallas_call(
        _post_attn_kernel,
        grid=(b, s // tile),
        in_specs=[tok(d), tok(DIFF_V_W), tok(MLA_V_W)] + [full(w) for w in weights],
        out_specs=[tok(d), tok(d),
                   pl.BlockSpec((1, 2 * PEER_HEADS, PEER_N_KEYS, tile), lambda bi, i: (bi, 0, 0, i))],
        out_shape=[jax.ShapeDtypeStruct((b, s, d), F32), jax.ShapeDtypeStruct((b, s, d), BF16),
                   jax.ShapeDtypeStruct((b, 2 * PEER_HEADS, PEER_N_KEYS, s), F32)],
        compiler_params=pltpu.CompilerParams(
            dimension_semantics=("parallel", "parallel"), vmem_limit_bytes=V7X_VMEM_LIMIT),
        name="post_attn",
    )(x, od, om, *weights)


_SUB = 8
LANES = 128
GATE_LANES = 256


def _top16_ranked(s, sub_idx, row16):
    rank = jnp.full(s.shape, float(PEER_N_KEYS - 1), F32)
    stacked = jnp.zeros(row16.shape, F32)
    vals = []
    for k in range(PEER_TOPK):
        m = jnp.max(s, axis=0, keepdims=True)
        first = jnp.min(jnp.where(s == m, sub_idx, float(PEER_N_KEYS)), axis=0, keepdims=True)
        sel = sub_idx == first
        rank = jnp.where(sel, float(k), rank)
        s = jnp.where(sel, -jnp.inf, s)
        stacked = jnp.where(row16 == float(k), m, stacked)
        vals.append(m)
    return vals, stacked, rank


_CODE_BASE = -(2.0 ** 126)
_CODE_STEP = 2.0 ** 121


def _top16_untied(s, row16):
    stacked = jnp.zeros(row16.shape, F32)
    vals = []
    for k in range(PEER_TOPK):
        m = jnp.max(s, axis=0, keepdims=True)
        s = jnp.where(s == m, _CODE_BASE - k * _CODE_STEP, s)
        stacked = jnp.where(row16 == float(k), m, stacked)
        vals.append(m)
    coded = s <= _CODE_BASE
    rank = jnp.where(coded, s * (-1.0 / _CODE_STEP) - 32.0, float(PEER_N_KEYS - 1))
    n_coded = jnp.sum(jnp.where(coded, 1.0, 0.0), axis=0, keepdims=True)
    return vals, stacked, rank, n_coded


def _dup_bf16(v):
    bits = lax.bitcast_convert_type(v.astype(BF16).astype(F32), jnp.uint32)
    return bits | (bits >> 16)


def _peer_topk_kernel(sc_ref, cnt_ref, e1_ref, r2_ref, e2_ref, *, td):
    sub_idx = lax.broadcasted_iota(jnp.int32, (PEER_N_KEYS, td), 0).astype(F32)
    row16 = lax.broadcasted_iota(jnp.int32, (PEER_TOPK, td), 0).astype(F32)
    row8 = lax.broadcasted_iota(jnp.int32, (_SUB, td), 0).astype(F32)
    neg_inf = jnp.full((_SUB, td), -jnp.inf, F32)
    k_top = float(PEER_TOPK)

    def make_head(exact_ties):
        def head(h):
            bad = jnp.zeros((1, td), F32)
            s1 = sc_ref[0, 2 * h]
            s2 = sc_ref[0, 2 * h + 1]
            if exact_ties:
                v1, _, rank1 = _top16_ranked(s1, sub_idx, row16)
                v2, v2s, rank2 = _top16_ranked(s2, sub_idx, row16)
            else:
                v1, _, rank1, n1 = _top16_untied(s1, row16)
                v2, v2s, rank2, n2 = _top16_untied(s2, row16)
                bad = jnp.maximum(bad, jnp.abs(n1 - k_top) + jnp.abs(n2 - k_top))

            cells, flats, n_invalid = [], [], []
            cells.append(v1[0] + v2s[:_SUB]); flats.append(row8); n_invalid.append(0)
            cells.append(v1[0] + v2s[_SUB:]); flats.append(row8 + float(_SUB)); n_invalid.append(0)
            for a in range(1, PEER_TOPK):
                nb = PEER_TOPK // (a + 1)
                cells.append(jnp.where(row8 < float(nb), v1[a] + v2s[:_SUB], neg_inf))
                flats.append(row8 + float(a * PEER_TOPK))
                n_invalid.append(_SUB - nb)
            big = float(PEER_TOPK * PEER_TOPK)
            top = None
            z = jnp.zeros((1, td), F32)
            for _ in range(PEER_TOPK):
                m = jnp.max(functools.reduce(jnp.maximum, cells), axis=0, keepdims=True)
                if exact_ties:
                    cand = [jnp.where(c == m, f, big) for c, f in zip(cells, flats)]
                    first = jnp.min(functools.reduce(jnp.minimum, cand), axis=0, keepdims=True)
                    cells = [jnp.where(f == first, -jnp.inf, c) for c, f in zip(cells, flats)]
                else:
                    cells = [jnp.where(c == m, -jnp.inf, c) for c in cells]
                if top is None:
                    top = m
                z = z + jnp.exp(m - top)
            knocked = [jnp.sum(jnp.where(c == -jnp.inf, 1.0, 0.0), axis=0, keepdims=True) - float(n)
                       for c, n in zip(cells, n_invalid)]
            counts = [knocked[0] + knocked[1]] + knocked[2:]
            if not exact_ties:
                bad = jnp.maximum(bad, jnp.abs(functools.reduce(jnp.add, counts) - k_top))

            cnt = jnp.zeros((PEER_N_KEYS, td), F32)
            for a in range(PEER_TOPK):
                cnt = jnp.where(rank1 == float(a), counts[a], cnt)
            e1 = jnp.exp(s1 - v1[0])
            e2 = jnp.exp(s2 - v2[0]) * (1.0 / z)
            for lt in range(td // LANES):
                tok = slice(lt * LANES, (lt + 1) * LANES)
                cnt_ref[0, lt, h] = _dup_bf16(cnt[:, tok])
                e1_ref[0, lt, h] = _dup_bf16(e1[:, tok])
            r2_ref[0, h] = rank2.astype(BF16)
            e2_ref[0, h] = e2.astype(BF16)
            return bad

        return head

    untied_head = make_head(False)
    exact_head = make_head(True)

    def head(h, carry):
        bad = untied_head(h)

        @pl.when(jnp.max(bad) > 0.0)
        def _():
            exact_head(h)

        return carry

    lax.fori_loop(0, PEER_HEADS, head, 0)


def _peer_topk(sc, *, tile):
    b, _, _, s = sc.shape
    row_spec = pl.BlockSpec((1, tile // LANES, PEER_HEADS, PEER_N_KEYS, LANES), lambda bi, i: (bi, i, 0, 0, 0))
    row_shape = jax.ShapeDtypeStruct((b, s // LANES, PEER_HEADS, PEER_N_KEYS, LANES), jnp.uint32)
    key_spec = pl.BlockSpec((1, PEER_HEADS, PEER_N_KEYS, tile), lambda bi, i: (bi, 0, 0, i))
    key_shape = jax.ShapeDtypeStruct((b, PEER_HEADS, PEER_N_KEYS, s), BF16)
    return pl.pallas_call(
        functools.partial(_peer_topk_kernel, td=tile),
        grid=(b, s // tile),
        in_specs=[pl.BlockSpec((1, 2 * PEER_HEADS, PEER_N_KEYS, tile), lambda bi, i: (bi, 0, 0, i))],
        out_specs=[row_spec, row_spec, key_spec, key_spec],
        out_shape=[row_shape, row_shape, key_shape, key_shape],
        compiler_params=pltpu.CompilerParams(
            dimension_semantics=("parallel", "parallel"), vmem_limit_bytes=V7X_VMEM_LIMIT),
        name="peer_topk",
    )(sc)


def _gelu_exact(x):
    return 0.5 * x * (1.0 + lax.erf(x * (1.0 / math.sqrt(2.0))))


def _packed_row(ref, lane_tiles, h, i1):
    def one(lt):
        row = jnp.broadcast_to(ref[0, lt, h, pl.ds(i1, 1), :], (PEER_N_KEYS // 2, LANES))
        return pltpu.bitcast(row, BF16)
    return jnp.concatenate([one(lt) for lt in lane_tiles], axis=1)


def _peer_ffn_kernel(xn_ref, u_ref, vt_ref, cnt_ref, e1_ref, r2_ref, e2_ref, x1_ref, o_ref,
                     acc_ref, gh_ref, *, rows_per_step):
    j = pl.program_id(2)

    @pl.when(j == 0)
    def _():
        acc_ref[...] = jnp.zeros_like(acc_ref)

    tt = xn_ref.shape[1]
    zero = jnp.zeros((PEER_N_KEYS, GATE_LANES), BF16)
    ht = _nt_dot(u_ref[...], xn_ref[0])
    act = _gelu_exact(ht).astype(BF16)
    for r in range(rows_per_step):
        i1 = j * rows_per_step + r
        keys = slice(r * PEER_N_KEYS, (r + 1) * PEER_N_KEYS)
        for c in range(tt // GATE_LANES):
            tok = slice(c * GATE_LANES, (c + 1) * GATE_LANES)
            lane_tiles = range(c * GATE_LANES // LANES, (c + 1) * GATE_LANES // LANES)
            gate = zero
            for h in range(PEER_HEADS):
                count = _packed_row(cnt_ref, lane_tiles, h, i1)
                e1 = _packed_row(e1_ref, lane_tiles, h, i1)
                gate = gate + jnp.where(r2_ref[0, h, :, tok] < count, e2_ref[0, h, :, tok], zero) * e1
            gh_ref[keys, tok] = gate * act[keys, tok]
    acc_ref[...] += _dot(vt_ref[...], gh_ref[...])

    @pl.when(j == pl.num_programs(2) - 1)
    def _():
        o_ref[0] = x1_ref[0] + acc_ref[...].T


def _peer_ffn(xn, x1, tables, u_bf, vt_bf, *, tile, experts_per_step):
    b, s, d = x1.shape
    cnt, e1, r2, e2 = tables
    rows = experts_per_step // PEER_N_KEYS
    tok = pl.BlockSpec((1, tile, d), lambda bi, i, j: (bi, i, 0))
    row_tab = pl.BlockSpec((1, tile // LANES, PEER_HEADS, PEER_N_KEYS, LANES), lambda bi, i, j: (bi, i, 0, 0, 0))
    key_tab = pl.BlockSpec((1, PEER_HEADS, PEER_N_KEYS, tile), lambda bi, i, j: (bi, 0, 0, i))
    return pl.pallas_call(
        functools.partial(_peer_ffn_kernel, rows_per_step=rows),
        grid=(b, s // tile, PEER_N_EXPERTS // experts_per_step),
        in_specs=[tok,
                  pl.BlockSpec((experts_per_step, d), lambda bi, i, j: (j, 0)),
                  pl.BlockSpec((d, experts_per_step), lambda bi, i, j: (0, j)),
                  row_tab, row_tab, key_tab, key_tab, tok],
        out_specs=tok,
        out_shape=jax.ShapeDtypeStruct((b, s, d), F32),
        scratch_shapes=[pltpu.VMEM((d, tile), F32), pltpu.VMEM((experts_per_step, tile), BF16)],
        compiler_params=pltpu.CompilerParams(
            dimension_semantics=("parallel", "parallel", "arbitrary"), vmem_limit_bytes=V7X_VMEM_LIMIT),
        name="peer_ffn",
    )(xn, u_bf, vt_bf, cnt, e1, r2, e2, x1)


def _col(v, reps=1, scale=1.0):
    return (jnp.tile(v.astype(F32), reps) * scale).reshape(-1, 1)


def _score_bound(scale_dim, gq, gk):
    bound = scale_dim * jnp.max(jnp.abs(gq.astype(F32))) * jnp.max(jnp.abs(gk.astype(F32))) * BF16_ROUNDING_MARGIN
    return (bound <= SCORE_BOUND_LOG2).astype(F32)


def _layer_params(l, norm_mix_g, w_in, diff_q_norm_g, diff_k_norm_g, lam_q1, lam_k1, lam_q2, lam_k2,
                  diff_subln_g, mla_q_latent_g, mla_w_uq, mla_kv_latent_g, mla_w_ukv, mla_q_norm_g,
                  mla_k_norm_g, w_out, norm_ffn_g, peer_w_q, peer_key1, peer_key2, peer_u, peer_v):
    lam_init = 0.8 - 0.6 * math.exp(-0.3 * l)
    lam = (jnp.exp(jnp.sum(lam_q1.astype(F32) * lam_k1.astype(F32)))
           - jnp.exp(jnp.sum(lam_q2.astype(F32) * lam_k2.astype(F32))) + lam_init)
    slopes = LOG2E * 2.0 ** (-8.0 * jnp.arange(1, N_DIFF_HEADS + 1, dtype=F32) / N_DIFF_HEADS)
    diff_scale = LOG2E * DIFF_QK_DIM ** -0.5
    mla_scale = LOG2E * MLA_QK_DIM ** -0.5
    diff_bound = _score_bound(diff_scale * DIFF_QK_DIM, diff_q_norm_g, diff_k_norm_g)
    mla_bound = _score_bound(mla_scale * MLA_QK_DIM, mla_q_norm_g, mla_k_norm_g)
    scal = jnp.concatenate([jnp.stack([lam, jnp.asarray(1.0 - lam_init, F32), diff_bound]), slopes]).astype(F32)
    keys = jnp.stack([peer_key1, peer_key2], axis=1).reshape(2 * PEER_HEADS, PEER_N_KEYS, PEER_HALF)
    return dict(
        gmix=norm_mix_g.reshape(1, -1), w_int=w_in.T.astype(BF16),
        gq=_col(diff_q_norm_g, 2 * N_DIFF_HEADS, diff_scale), gk=_col(diff_k_norm_g, 2 * N_DIFF_HEADS),
        gcq=_col(mla_q_latent_g), w_uqt=mla_w_uq.T.astype(BF16),
        gckv=_col(mla_kv_latent_g), w_ukvt=mla_w_ukv.T.astype(BF16),
        gmq=_col(mla_q_norm_g, 1, mla_scale), gmk=_col(mla_k_norm_g),
        scal=scal, mla_scal=mla_bound.reshape(1), gsub=_col(diff_subln_g),
        w_out=w_out.astype(BF16), gffn=norm_ffn_g.reshape(1, -1), w_qt=peer_w_q.T.astype(BF16),
        keys=keys.astype(BF16), u=peer_u.astype(BF16), vt=peer_v.T.astype(BF16),
    )


def _rope_tables(s):
    half = MLA_ROPE_DIM // 2
    inv = ROPE_THETA ** (-jnp.arange(half, dtype=F32) / half)
    ang = inv[:, None] * jnp.arange(s, dtype=F32)[None, :]
    return jnp.cos(ang), jnp.sin(ang)


def _tiles(s):
    return dict(proj=min(512, s), tq=min(512, s), tk=min(512, s), topk=min(256, s), ffn=min(512, s))


def _layer(x, p):
    s = x.shape[1]
    t = _tiles(s)
    cos_t, sin_t = _rope_tables(s)
    qd, kd, vd, qm, km, vm = _pre_attn(x, p, cos_t, sin_t, tile=t["proj"])
    od = _diff_attn(p["scal"], qd, kd, vd, p["gsub"], tq=t["tq"], tk=t["tk"])
    om = _mla_attn(p["mla_scal"], qm, km, vm, tq=t["tq"], tk=t["tk"])
    x1, xn, sc = _post_attn(x, od, om, p, tile=t["proj"])
    tables = _peer_topk(sc, tile=t["topk"])
    return _peer_ffn(xn, x1, tables, p["u"], p["vt"], tile=t["ffn"], experts_per_step=2048)


def kernel(x_prompt, x_sample, norm_mix_g, w_in, diff_q_norm_g, diff_k_norm_g, lam_q1, lam_k1, lam_q2, lam_k2,
           diff_subln_g, mla_q_latent_g, mla_w_uq, mla_kv_latent_g, mla_w_ukv, mla_q_norm_g, mla_k_norm_g,
           w_out, norm_ffn_g, peer_w_q, peer_key1, peer_key2, peer_u, peer_v):
    stacked = (norm_mix_g, w_in, diff_q_norm_g, diff_k_norm_g, lam_q1, lam_k1, lam_q2, lam_k2, diff_subln_g,
               mla_q_latent_g, mla_w_uq, mla_kv_latent_g, mla_w_ukv, mla_q_norm_g, mla_k_norm_g, w_out,
               norm_ffn_g, peer_w_q, peer_key1, peer_key2, peer_u, peer_v)
    y_prompt, y_sample = x_prompt, x_sample
    for l in range(DEPTH):
        p = _layer_params(l, *(w[l] for w in stacked))
        y_prompt = _layer(y_prompt, p)
        y_sample = _layer(y_sample, p)
    return (y_prompt, y_sample)
```

```python
import functools
import math

import jax
import jax.numpy as jnp
from jax import lax
from jax.experimental import pallas as pl
from jax.experimental.pallas import tpu as pltpu

F32 = jnp.float32
BF16 = jnp.bfloat16

D_MODEL = 1024
DEPTH = 2
N_DIFF_HEADS = 4
DIFF_QK_DIM = 64
DIFF_V_DIM = 128
N_MLA_HEADS = 4
MLA_Q_RANK = 256
MLA_KV_RANK = 256
MLA_NOPE_DIM = 64
MLA_ROPE_DIM = 32
MLA_QK_DIM = MLA_NOPE_DIM + MLA_ROPE_DIM
MLA_V_DIM = 128
ROPE_THETA = 10000.0
DIFF_Q_W = N_DIFF_HEADS * 2 * DIFF_QK_DIM
DIFF_V_W = N_DIFF_HEADS * DIFF_V_DIM
MLA_V_W = N_MLA_HEADS * MLA_V_DIM
IN_WIDTH = 3 * DIFF_Q_W + MLA_Q_RANK + MLA_KV_RANK + MLA_ROPE_DIM
PEER_HEADS = 8
PEER_N_KEYS = 128
PEER_N_EXPERTS = PEER_N_KEYS * PEER_N_KEYS
PEER_HALF = 128
PEER_TOPK = 16
EPS = 1e-6

LOG2E = 1.0 / math.log(2.0)
SCORE_BOUND_LOG2 = 60.0
BF16_ROUNDING_MARGIN = 1.02

KV_UNROLL = 2

HEAD_LANES = 128
V7X_VMEM_LIMIT = 56 * 1024 * 1024

_OFF_DQ = 0
_OFF_DK = DIFF_Q_W
_OFF_DV = 2 * DIFF_Q_W
_OFF_CQ = 3 * DIFF_Q_W
_OFF_CKV = _OFF_CQ + MLA_Q_RANK
_OFF_KR = _OFF_CKV + MLA_KV_RANK


def _nt_dot(a, b):
    return lax.dot_general(a, b, (((1,), (1,)), ((), ())), preferred_element_type=F32)


def _dot(a, b):
    return jnp.dot(a, b, preferred_element_type=F32)


def _group_rmsnorm_fm(a, n_groups, width, g_col):
    t = a.shape[1]
    a3 = a.reshape(n_groups, width, t)
    ms = jnp.mean(a3 * a3, axis=1, keepdims=True)
    return (a3 * lax.rsqrt(ms + EPS)).reshape(n_groups * width, t) * g_col


def _rope_fm(r, cos, sin):
    half = MLA_ROPE_DIM // 2
    r1, r2 = r[:half], r[half:]
    return r1 * cos - r2 * sin, r2 * cos + r1 * sin


def _pre_attn_kernel(x_ref, gmix_ref, wint_ref, gq_ref, gk_ref, gcq_ref, wuqt_ref, gckv_ref,
                     wukvt_ref, gmq_ref, gmk_ref, cos_ref, sin_ref,
                     qd_ref, kd_ref, vd_ref, qm_ref, km_ref, vm_ref, kt_scr):
    x = x_ref[0]
    t = x.shape[0]
    ms = jnp.mean(x * x, axis=-1, keepdims=True)
    xn = (x * lax.rsqrt(ms + EPS) * gmix_ref[...]).astype(BF16)
    ht = _nt_dot(wint_ref[...], xn)
    cos = cos_ref[...]
    sin = sin_ref[...]

    dq = _group_rmsnorm_fm(ht[_OFF_DQ:_OFF_DQ + DIFF_Q_W], 2 * N_DIFF_HEADS, DIFF_QK_DIM, gq_ref[...])
    qd_ref[0] = dq.astype(BF16)
    dk = _group_rmsnorm_fm(ht[_OFF_DK:_OFF_DK + DIFF_Q_W], 2 * N_DIFF_HEADS, DIFF_QK_DIM, gk_ref[...])
    kd_ref[0] = dk.T.astype(BF16)
    vd_ref[0] = ht[_OFF_DV:_OFF_DV + DIFF_V_W].astype(BF16)

    cq = ht[_OFF_CQ:_OFF_CQ + MLA_Q_RANK]
    cqn = (cq * lax.rsqrt(jnp.mean(cq * cq, axis=0, keepdims=True) + EPS) * gcq_ref[...]).astype(BF16)
    mq = _dot(wuqt_ref[...], cqn)
    pad = jnp.zeros((HEAD_LANES - MLA_QK_DIM, t), BF16)
    for h in range(N_MLA_HEADS):
        m = mq[h * MLA_QK_DIM:(h + 1) * MLA_QK_DIM]
        mn = m * lax.rsqrt(jnp.mean(m * m, axis=0, keepdims=True) + EPS) * gmq_ref[...]
        o1, o2 = _rope_fm(mn[MLA_NOPE_DIM:], cos, sin)
        base = h * HEAD_LANES
        qm_ref[0, base:base + MLA_NOPE_DIM] = mn[:MLA_NOPE_DIM].astype(BF16)
        qm_ref[0, base + MLA_NOPE_DIM:base + MLA_NOPE_DIM + 16] = o1.astype(BF16)
        qm_ref[0, base + MLA_NOPE_DIM + 16:base + MLA_QK_DIM] = o2.astype(BF16)
        qm_ref[0, base + MLA_QK_DIM:base + HEAD_LANES] = pad

    ckv = ht[_OFF_CKV:_OFF_CKV + MLA_KV_RANK]
    ckvn = (ckv * lax.rsqrt(jnp.mean(ckv * ckv, axis=0, keepdims=True) + EPS) * gckv_ref[...]).astype(BF16)
    kv = _dot(wukvt_ref[...], ckvn)
    kr = ht[_OFF_KR:_OFF_KR + MLA_ROPE_DIM]
    kr_ss = jnp.sum(kr * kr, axis=0, keepdims=True)
    gmk = gmk_ref[...]
    per_head = MLA_NOPE_DIM + MLA_V_DIM
    for h in range(N_MLA_HEADS):
        kn = kv[h * per_head:h * per_head + MLA_NOPE_DIM]
        v = kv[h * per_head + MLA_NOPE_DIM:(h + 1) * per_head]
        ms_k = (jnp.sum(kn * kn, axis=0, keepdims=True) + kr_ss) * (1.0 / MLA_QK_DIM)
        inv = lax.rsqrt(ms_k + EPS)
        o1, o2 = _rope_fm(kr * inv * gmk[MLA_NOPE_DIM:], cos, sin)
        base = h * HEAD_LANES
        kt_scr[base:base + MLA_NOPE_DIM] = kn * inv * gmk[:MLA_NOPE_DIM]
        kt_scr[base + MLA_NOPE_DIM:base + MLA_NOPE_DIM + 16] = o1
        kt_scr[base + MLA_NOPE_DIM + 16:base + MLA_QK_DIM] = o2
        kt_scr[base + MLA_QK_DIM:base + HEAD_LANES] = jnp.zeros((HEAD_LANES - MLA_QK_DIM, t), F32)
        vm_ref[0, base:base + HEAD_LANES] = v.astype(BF16)
    km_ref[0] = kt_scr[...].T.astype(BF16)


def _pre_attn(x, p, cos_t, sin_t, *, tile):
    b, s, d = x.shape
    hw = N_DIFF_HEADS * HEAD_LANES
    full = lambda a: pl.BlockSpec(a.shape, lambda bi, i: (0,) * a.ndim)
    fm_spec = pl.BlockSpec((1, hw, tile), lambda bi, i: (bi, 0, i))
    tm_spec = pl.BlockSpec((1, tile, hw), lambda bi, i: (bi, i, 0))
    weights = (p["gmix"], p["w_int"], p["gq"], p["gk"], p["gcq"], p["w_uqt"], p["gckv"], p["w_ukvt"],
               p["gmq"], p["gmk"])
    rope_spec = pl.BlockSpec((MLA_ROPE_DIM // 2, tile), lambda bi, i: (0, i))
    fm = jax.ShapeDtypeStruct((b, hw, s), BF16)
    tm = jax.ShapeDtypeStruct((b, s, hw), BF16)
    return pl.pallas_call(
        _pre_attn_kernel,
        grid=(b, s // tile),
        in_specs=[pl.BlockSpec((1, tile, d), lambda bi, i: (bi, i, 0))] + [full(w) for w in weights]
        + [rope_spec, rope_spec],
        out_specs=[fm_spec, tm_spec, fm_spec, fm_spec, tm_spec, fm_spec],
        out_shape=[fm, tm, fm, fm, tm, fm],
        scratch_shapes=[pltpu.VMEM((hw, tile), F32)],
        compiler_params=pltpu.CompilerParams(
            dimension_semantics=("parallel", "parallel"), vmem_limit_bytes=V7X_VMEM_LIMIT),
        name="pre_attn",
    )(x, *weights, cos_t, sin_t)


def _online_softmax_step(k, vt, qp, bias, m, l, acc_ref):
    s = _dot(k, qp)
    if bias is not None:
        s = s - bias
    mn = jnp.maximum(m, jnp.max(s, axis=0, keepdims=True))
    alpha = jnp.exp2(m - mn)
    p = jnp.exp2(s - mn)
    l = alpha * l + jnp.sum(p, axis=0, keepdims=True)
    acc_ref[...] = alpha * acc_ref[...] + _dot(vt, p.astype(BF16))
    return mn, l


def _bounded_softmax_step(k, vt, qp, bias, l, acc_ref):
    s = _dot(k, qp)
    if bias is not None:
        s = s - bias
    p = jnp.exp2(s)
    acc_ref[...] += _dot(vt, p.astype(BF16))
    return l + jnp.sum(p, axis=0, keepdims=True)


def _kv_chunk(k_ref, vt_ref, j, tk):
    off = pl.multiple_of(j * tk, tk)
    return k_ref[0, pl.ds(off, tk), :], vt_ref[0, :, pl.ds(off, tk)]


def _diff_attn_kernel(scal_ref, qt_ref, k_ref, vt_ref, g_ref, o_ref, acc0_ref, acc1_ref, *, tq, tk, n_kv):
    h = pl.program_id(1)
    i = pl.program_id(2)
    lam = scal_ref[0]
    sub_scale = scal_ref[1]
    bounded = scal_ref[2] > 0.5
    slope = scal_ref[3 + h]
    q = qt_ref[0]
    row = lax.broadcasted_iota(jnp.int32, q.shape, 0)
    zero = jnp.zeros_like(q)
    q0 = jnp.where(row < DIFF_QK_DIM, q, zero)
    q1 = jnp.where(row >= DIFF_QK_DIM, q, zero)
    d0 = (i * tq + lax.broadcasted_iota(jnp.int32, (tk, tq), 1)
          - lax.broadcasted_iota(jnp.int32, (tk, tq), 0)).astype(F32)
    acc0_ref[...] = jnp.zeros_like(acc0_ref)
    acc1_ref[...] = jnp.zeros_like(acc1_ref)

    neg = jnp.full((1, tq), -jnp.inf, F32)
    zer = jnp.zeros((1, tq), F32)

    def finish(l0, l1):
        o = acc0_ref[...] / l0 - lam * (acc1_ref[...] / l1)
        o = o * lax.rsqrt(jnp.mean(o * o, axis=0, keepdims=True) + EPS) * g_ref[...] * sub_scale
        o_ref[0] = o.T.astype(BF16)

    @pl.when(bounded)
    def _():
        def body(j, carry):
            l0, l1 = carry
            k, vt = _kv_chunk(k_ref, vt_ref, j, tk)
            bias = slope * jnp.abs(d0 - lax.convert_element_type(j * tk, F32))
            l0 = _bounded_softmax_step(k, vt, q0, bias, l0, acc0_ref)
            l1 = _bounded_softmax_step(k, vt, q1, bias, l1, acc1_ref)
            return l0, l1

        finish(*lax.fori_loop(0, n_kv, body, (zer, zer), unroll=KV_UNROLL))

    @pl.when(jnp.logical_not(bounded))
    def _():
        def body(j, carry):
            m0, l0, m1, l1 = carry
            k, vt = _kv_chunk(k_ref, vt_ref, j, tk)
            bias = slope * jnp.abs(d0 - lax.convert_element_type(j * tk, F32))
            m0, l0 = _online_softmax_step(k, vt, q0, bias, m0, l0, acc0_ref)
            m1, l1 = _online_softmax_step(k, vt, q1, bias, m1, l1, acc1_ref)
            return m0, l0, m1, l1

        _, l0, _, l1 = lax.fori_loop(0, n_kv, body, (neg, zer, neg, zer))
        finish(l0, l1)


def _mla_attn_kernel(scal_ref, qt_ref, k_ref, vt_ref, o_ref, acc_ref, *, tq, tk, n_kv):
    bounded = scal_ref[0] > 0.5
    q = qt_ref[0]
    acc_ref[...] = jnp.zeros_like(acc_ref)
    zer = jnp.zeros((1, tq), F32)

    def finish(l):
        o_ref[0] = (acc_ref[...] / l).T.astype(BF16)

    @pl.when(bounded)
    def _():
        def body(j, l):
            k, vt = _kv_chunk(k_ref, vt_ref, j, tk)
            return _bounded_softmax_step(k, vt, q, None, l, acc_ref)

        finish(lax.fori_loop(0, n_kv, body, zer, unroll=KV_UNROLL))

    @pl.when(jnp.logical_not(bounded))
    def _():
        def body(j, carry):
            k, vt = _kv_chunk(k_ref, vt_ref, j, tk)
            return _online_softmax_step(k, vt, q, None, *carry, acc_ref)

        _, l = lax.fori_loop(0, n_kv, body, (jnp.full((1, tq), -jnp.inf, F32), zer))
        finish(l)


def _attn_specs(s, tq):
    q_spec = pl.BlockSpec((1, HEAD_LANES, tq), lambda bi, h, i: (bi, h, i))
    k_spec = pl.BlockSpec((1, s, HEAD_LANES), lambda bi, h, i: (bi, 0, h))
    v_spec = pl.BlockSpec((1, HEAD_LANES, s), lambda bi, h, i: (bi, h, 0))
    o_spec = pl.BlockSpec((1, tq, HEAD_LANES), lambda bi, h, i: (bi, i, h))
    return q_spec, k_spec, v_spec, o_spec


def _diff_attn(scal, qt, k, vt, g_col, *, tq, tk):
    b, hw, s = qt.shape
    q_spec, k_spec, v_spec, o_spec = _attn_specs(s, tq)
    return pl.pallas_call(
        functools.partial(_diff_attn_kernel, tq=tq, tk=tk, n_kv=s // tk),
        grid=(b, N_DIFF_HEADS, s // tq),
        in_specs=[pl.BlockSpec(memory_space=pltpu.SMEM), q_spec, k_spec, v_spec,
                  pl.BlockSpec(g_col.shape, lambda bi, h, i: (0, 0))],
        out_specs=o_spec,
        out_shape=jax.ShapeDtypeStruct((b, s, hw), BF16),
        scratch_shapes=[pltpu.VMEM((HEAD_LANES, tq), F32), pltpu.VMEM((HEAD_LANES, tq), F32)],
        compiler_params=pltpu.CompilerParams(
            dimension_semantics=("parallel", "parallel", "parallel"), vmem_limit_bytes=V7X_VMEM_LIMIT),
        name="diff_attn",
    )(scal, qt, k, vt, g_col)


def _mla_attn(scal, qt, k, vt, *, tq, tk):
    b, hw, s = qt.shape
    q_spec, k_spec, v_spec, o_spec = _attn_specs(s, tq)
    return pl.pallas_call(
        functools.partial(_mla_attn_kernel, tq=tq, tk=tk, n_kv=s // tk),
        grid=(b, N_MLA_HEADS, s // tq),
        in_specs=[pl.BlockSpec(memory_space=pltpu.SMEM), q_spec, k_spec, v_spec],
        out_specs=o_spec,
        out_shape=jax.ShapeDtypeStruct((b, s, hw), BF16),
        scratch_shapes=[pltpu.VMEM((HEAD_LANES, tq), F32)],
        compiler_params=pltpu.CompilerParams(
            dimension_semantics=("parallel", "parallel", "parallel"), vmem_limit_bytes=V7X_VMEM_LIMIT),
        name="mla_attn",
    )(scal, qt, k, vt)


def _post_attn_kernel(x_ref, od_ref, om_ref, wout_ref, gffn_ref, wqt_ref, keys_ref,
                      x1_ref, xn_ref, sc_ref):
    x1 = (x_ref[0] + _dot(od_ref[0], wout_ref[:DIFF_V_W]) + _dot(om_ref[0], wout_ref[DIFF_V_W:]))
    x1_ref[0] = x1
    ms = jnp.mean(x1 * x1, axis=-1, keepdims=True)
    xn = (x1 * lax.rsqrt(ms + EPS) * gffn_ref[...]).astype(BF16)
    xn_ref[0] = xn
    qt = _nt_dot(wqt_ref[...], xn).astype(BF16)
    for hs in range(2 * PEER_HEADS):
        sc_ref[0, hs] = _dot(keys_ref[hs], qt[hs * PEER_HALF:(hs + 1) * PEER_HALF])


def _post_attn(x, od, om, p, *, tile):
    b, s, d = x.shape
    full = lambda a: pl.BlockSpec(a.shape, lambda bi, i: (0,) * a.ndim)
    tok = lambda w: pl.BlockSpec((1, tile, w), lambda bi, i: (bi, i, 0))
    weights = (p["w_out"], p["gffn"], p["w_qt"], p["keys"])
    return pl.pallas_call(
        _post_attn_kernel,
        grid=(b, s // tile),
        in_specs=[tok(d), tok(DIFF_V_W), tok(MLA_V_W)] + [full(w) for w in weights],
        out_specs=[tok(d), tok(d),
                   pl.BlockSpec((1, 2 * PEER_HEADS, PEER_N_KEYS, tile), lambda bi, i: (bi, 0, 0, i))],
        out_shape=[jax.ShapeDtypeStruct((b, s, d), F32), jax.ShapeDtypeStruct((b, s, d), BF16),
                   jax.ShapeDtypeStruct((b, 2 * PEER_HEADS, PEER_N_KEYS, s), F32)],
        compiler_params=pltpu.CompilerParams(
            dimension_semantics=("parallel", "parallel"), vmem_limit_bytes=V7X_VMEM_LIMIT),
        name="post_attn",
    )(x, od, om, *weights)


_SUB = 8
LANES = 128
GATE_LANES = 256


def _top16_ranked(s, sub_idx, row16):
    rank = jnp.full(s.shape, float(PEER_N_KEYS - 1), F32)
    stacked = jnp.zeros(row16.shape, F32)
    vals = []
    for k in range(PEER_TOPK):
        m = jnp.max(s, axis=0, keepdims=True)
        first = jnp.min(jnp.where(s == m, sub_idx, float(PEER_N_KEYS)), axis=0, keepdims=True)
        sel = sub_idx == first
        rank = jnp.where(sel, float(k), rank)
        s = jnp.where(sel, -jnp.inf, s)
        stacked = jnp.where(row16 == float(k), m, stacked)
        vals.append(m)
    return vals, stacked, rank


_CODE_BASE = -(2.0 ** 126)
_CODE_STEP = 2.0 ** 121


def _top16_untied(s, row16):
    stacked = jnp.zeros(row16.shape, F32)
    vals = []
    for k in range(PEER_TOPK):
        m = jnp.max(s, axis=0, keepdims=True)
        s = jnp.where(s == m, _CODE_BASE - k * _CODE_STEP, s)
        stacked = jnp.where(row16 == float(k), m, stacked)
        vals.append(m)
    coded = s <= _CODE_BASE
    rank = jnp.where(coded, s * (-1.0 / _CODE_STEP) - 32.0, float(PEER_N_KEYS - 1))
    n_coded = jnp.sum(jnp.where(coded, 1.0, 0.0), axis=0, keepdims=True)
    return vals, stacked, rank, n_coded


def _dup_bf16(v):
    bits = lax.bitcast_convert_type(v.astype(BF16).astype(F32), jnp.uint32)
    return bits | (bits >> 16)


def _peer_topk_kernel(sc_ref, cnt_ref, e1_ref, r2_ref, e2_ref, *, td):
    sub_idx = lax.broadcasted_iota(jnp.int32, (PEER_N_KEYS, td), 0).astype(F32)
    row16 = lax.broadcasted_iota(jnp.int32, (PEER_TOPK, td), 0).astype(F32)
    row8 = lax.broadcasted_iota(jnp.int32, (_SUB, td), 0).astype(F32)
    neg_inf = jnp.full((_SUB, td), -jnp.inf, F32)
    k_top = float(PEER_TOPK)

    def make_head(exact_ties):
        def head(h):
            bad = jnp.zeros((1, td), F32)
            s1 = sc_ref[0, 2 * h]
            s2 = sc_ref[0, 2 * h + 1]
            if exact_ties:
                v1, _, rank1 = _top16_ranked(s1, sub_idx, row16)
                v2, v2s, rank2 = _top16_ranked(s2, sub_idx, row16)
            else:
                v1, _, rank1, n1 = _top16_untied(s1, row16)
                v2, v2s, rank2, n2 = _top16_untied(s2, row16)
                bad = jnp.maximum(bad, jnp.abs(n1 - k_top) + jnp.abs(n2 - k_top))

            cells, flats, n_invalid = [], [], []
            cells.append(v1[0] + v2s[:_SUB]); flats.append(row8); n_invalid.append(0)
            cells.append(v1[0] + v2s[_SUB:]); flats.append(row8 + float(_SUB)); n_invalid.append(0)
            for a in range(1, PEER_TOPK):
                nb = PEER_TOPK // (a + 1)
                cells.append(jnp.where(row8 < float(nb), v1[a] + v2s[:_SUB], neg_inf))
                flats.append(row8 + float(a * PEER_TOPK))
                n_invalid.append(_SUB - nb)
            big = float(PEER_TOPK * PEER_TOPK)
            top = None
            z = jnp.zeros((1, td), F32)
            for _ in range(PEER_TOPK):
                m = jnp.max(functools.reduce(jnp.maximum, cells), axis=0, keepdims=True)
                if exact_ties:
                    cand = [jnp.where(c == m, f, big) for c, f in zip(cells, flats)]
                    first = jnp.min(functools.reduce(jnp.minimum, cand), axis=0, keepdims=True)
                    cells = [jnp.where(f == first, -jnp.inf, c) for c, f in zip(cells, flats)]
                else:
                    cells = [jnp.where(c == m, -jnp.inf, c) for c in cells]
                if top is None:
                    top = m
                z = z + jnp.exp(m - top)
            knocked = [jnp.sum(jnp.where(c == -jnp.inf, 1.0, 0.0), axis=0, keepdims=True) - float(n)
                       for c, n in zip(cells, n_invalid)]
            counts = [knocked[0] + knocked[1]] + knocked[2:]
            if not exact_ties:
                bad = jnp.maximum(bad, jnp.abs(functools.reduce(jnp.add, counts) - k_top))

            cnt = jnp.zeros((PEER_N_KEYS, td), F32)
            for a in range(PEER_TOPK):
                cnt = jnp.where(rank1 == float(a), counts[a], cnt)
            e1 = jnp.exp(s1 - v1[0])
            e2 = jnp.exp(s2 - v2[0]) * (1.0 / z)
            for lt in range(td // LANES):
                tok = slice(lt * LANES, (lt + 1) * LANES)
                cnt_ref[0, lt, h] = _dup_bf16(cnt[:, tok])
                e1_ref[0, lt, h] = _dup_bf16(e1[:, tok])
            r2_ref[0, h] = rank2.astype(BF16)
            e2_ref[0, h] = e2.astype(BF16)
            return bad

        return head

    untied_head = make_head(False)
    exact_head = make_head(True)

    def head(h, carry):
        bad = untied_head(h)

        @pl.when(jnp.max(bad) > 0.0)
        def _():
            exact_head(h)

        return carry

    lax.fori_loop(0, PEER_HEADS, head, 0)


def _peer_topk(sc, *, tile):
    b, _, _, s = sc.shape
    row_spec = pl.BlockSpec((1, tile // LANES, PEER_HEADS, PEER_N_KEYS, LANES), lambda bi, i: (bi, i, 0, 0, 0))
    row_shape = jax.ShapeDtypeStruct((b, s // LANES, PEER_HEADS, PEER_N_KEYS, LANES), jnp.uint32)
    key_spec = pl.BlockSpec((1, PEER_HEADS, PEER_N_KEYS, tile), lambda bi, i: (bi, 0, 0, i))
    key_shape = jax.ShapeDtypeStruct((b, PEER_HEADS, PEER_N_KEYS, s), BF16)
    return pl.pallas_call(
        functools.partial(_peer_topk_kernel, td=tile),
        grid=(b, s // tile),
        in_specs=[pl.BlockSpec((1, 2 * PEER_HEADS, PEER_N_KEYS, tile), lambda bi, i: (bi, 0, 0, i))],
        out_specs=[row_spec, row_spec, key_spec, key_spec],
        out_shape=[row_shape, row_shape, key_shape, key_shape],
        compiler_params=pltpu.CompilerParams(
            dimension_semantics=("parallel", "parallel"), vmem_limit_bytes=V7X_VMEM_LIMIT),
        name="peer_topk",
    )(sc)


def _gelu_exact(x):
    return 0.5 * x * (1.0 + lax.erf(x * (1.0 / math.sqrt(2.0))))


def _packed_row(ref, lane_tiles, h, i1):
    def one(lt):
        row = jnp.broadcast_to(ref[0, lt, h, pl.ds(i1, 1), :], (PEER_N_KEYS // 2, LANES))
        return pltpu.bitcast(row, BF16)
    return jnp.concatenate([one(lt) for lt in lane_tiles], axis=1)


def _peer_ffn_kernel(xn_ref, u_ref, vt_ref, cnt_ref, e1_ref, r2_ref, e2_ref, x1_ref, o_ref,
                     acc_ref, gh_ref, *, rows_per_step):
    j = pl.program_id(2)

    @pl.when(j == 0)
    def _():
        acc_ref[...] = jnp.zeros_like(acc_ref)

    tt = xn_ref.shape[1]
    zero = jnp.zeros((PEER_N_KEYS, GATE_LANES), BF16)
    ht = _nt_dot(u_ref[...], xn_ref[0])
    act = _gelu_exact(ht).astype(BF16)
    for r in range(rows_per_step):
        i1 = j * rows_per_step + r
        keys = slice(r * PEER_N_KEYS, (r + 1) * PEER_N_KEYS)
        for c in range(tt // GATE_LANES):
            tok = slice(c * GATE_LANES, (c + 1) * GATE_LANES)
            lane_tiles = range(c * GATE_LANES // LANES, (c + 1) * GATE_LANES // LANES)
            gate = zero
            for h in range(PEER_HEADS):
                count = _packed_row(cnt_ref, lane_tiles, h, i1)
                e1 = _packed_row(e1_ref, lane_tiles, h, i1)
                gate = gate + jnp.where(r2_ref[0, h, :, tok] < count, e2_ref[0, h, :, tok], zero) * e1
            gh_ref[keys, tok] = gate * act[keys, tok]
    acc_ref[...] += _dot(vt_ref[...], gh_ref[...])

    @pl.when(j == pl.num_programs(2) - 1)
    def _():
        o_ref[0] = x1_ref[0] + acc_ref[...].T


def _peer_ffn(xn, x1, tables, u_bf, vt_bf, *, tile, experts_per_step):
    b, s, d = x1.shape
    cnt, e1, r2, e2 = tables
    rows = experts_per_step // PEER_N_KEYS
    tok = pl.BlockSpec((1, tile, d), lambda bi, i, j: (bi, i, 0))
    row_tab = pl.BlockSpec((1, tile // LANES, PEER_HEADS, PEER_N_KEYS, LANES), lambda bi, i, j: (bi, i, 0, 0, 0))
    key_tab = pl.BlockSpec((1, PEER_HEADS, PEER_N_KEYS, tile), lambda bi, i, j: (bi, 0, 0, i))
    return pl.pallas_call(
        functools.partial(_peer_ffn_kernel, rows_per_step=rows),
        grid=(b, s // tile, PEER_N_EXPERTS // experts_per_step),
        in_specs=[tok,
                  pl.BlockSpec((experts_per_step, d), lambda bi, i, j: (j, 0)),
                  pl.BlockSpec((d, experts_per_step), lambda bi, i, j: (0, j)),
                  row_tab, row_tab, key_tab, key_tab, tok],
        out_specs=tok,
        out_shape=jax.ShapeDtypeStruct((b, s, d), F32),
        scratch_shapes=[pltpu.VMEM((d, tile), F32), pltpu.VMEM((experts_per_step, tile), BF16)],
        compiler_params=pltpu.CompilerParams(
            dimension_semantics=("parallel", "parallel", "arbitrary"), vmem_limit_bytes=V7X_VMEM_LIMIT),
        name="peer_ffn",
    )(xn, u_bf, vt_bf, cnt, e1, r2, e2, x1)


def _col(v, reps=1, scale=1.0):
    return (jnp.tile(v.astype(F32), reps) * scale).reshape(-1, 1)


def _score_bound(scale_dim, gq, gk):
    bound = scale_dim * jnp.max(jnp.abs(gq.astype(F32))) * jnp.max(jnp.abs(gk.astype(F32))) * BF16_ROUNDING_MARGIN
    return (bound <= SCORE_BOUND_LOG2).astype(F32)


def _layer_params(l, norm_mix_g, w_in, diff_q_norm_g, diff_k_norm_g, lam_q1, lam_k1, lam_q2, lam_k2,
                  diff_subln_g, mla_q_latent_g, mla_w_uq, mla_kv_latent_g, mla_w_ukv, mla_q_norm_g,
                  mla_k_norm_g, w_out, norm_ffn_g, peer_w_q, peer_key1, peer_key2, peer_u, peer_v):
    lam_init = 0.8 - 0.6 * math.exp(-0.3 * l)
    lam = (jnp.exp(jnp.sum(lam_q1.astype(F32) * lam_k1.astype(F32)))
           - jnp.exp(jnp.sum(lam_q2.astype(F32) * lam_k2.astype(F32))) + lam_init)
    slopes = LOG2E * 2.0 ** (-8.0 * jnp.arange(1, N_DIFF_HEADS + 1, dtype=F32) / N_DIFF_HEADS)
    diff_scale = LOG2E * DIFF_QK_DIM ** -0.5
    mla_scale = LOG2E * MLA_QK_DIM ** -0.5
    diff_bound = _score_bound(diff_scale * DIFF_QK_DIM, diff_q_norm_g, diff_k_norm_g)
    mla_bound = _score_bound(mla_scale * MLA_QK_DIM, mla_q_norm_g, mla_k_norm_g)
    scal = jnp.concatenate([jnp.stack([lam, jnp.asarray(1.0 - lam_init, F32), diff_bound]), slopes]).astype(F32)
    keys = jnp.stack([peer_key1, peer_key2], axis=1).reshape(2 * PEER_HEADS, PEER_N_KEYS, PEER_HALF)
    return dict(
        gmix=norm_mix_g.reshape(1, -1), w_int=w_in.T.astype(BF16),
        gq=_col(diff_q_norm_g, 2 * N_DIFF_HEADS, diff_scale), gk=_col(diff_k_norm_g, 2 * N_DIFF_HEADS),
        gcq=_col(mla_q_latent_g), w_uqt=mla_w_uq.T.astype(BF16),
        gckv=_col(mla_kv_latent_g), w_ukvt=mla_w_ukv.T.astype(BF16),
        gmq=_col(mla_q_norm_g, 1, mla_scale), gmk=_col(mla_k_norm_g),
        scal=scal, mla_scal=mla_bound.reshape(1), gsub=_col(diff_subln_g),
        w_out=w_out.astype(BF16), gffn=norm_ffn_g.reshape(1, -1), w_qt=peer_w_q.T.astype(BF16),
        keys=keys.astype(BF16), u=peer_u.astype(BF16), vt=peer_v.T.astype(BF16),
    )


def _rope_tables(s):
    half = MLA_ROPE_DIM // 2
    inv = ROPE_THETA ** (-jnp.arange(half, dtype=F32) / half)
    ang = inv[:, None] * jnp.arange(s, dtype=F32)[None, :]
    return jnp.cos(ang), jnp.sin(ang)


def _tiles(s):
    return dict(proj=min(512, s), tq=min(512, s), tk_diff=min(1024, s), tk_mla=min(2048, s), topk=min(256, s), ffn=min(512, s))


def _layer(x, p):
    s = x.shape[1]
    t = _tiles(s)
    cos_t, sin_t = _rope_tables(s)
    qd, kd, vd, qm, km, vm = _pre_attn(x, p, cos_t, sin_t, tile=t["proj"])
    od = _diff_attn(p["scal"], qd, kd, vd, p["gsub"], tq=t["tq"], tk=t["tk_diff"])
    om = _mla_attn(p["mla_scal"], qm, km, vm, tq=t["tq"], tk=t["tk_mla"])
    x1, xn, sc = _post_attn(x, od, om, p, tile=t["proj"])
    tables = _peer_topk(sc, tile=t["topk"])
    return _peer_ffn(xn, x1, tables, p["u"], p["vt"], tile=t["ffn"], experts_per_step=2048)


def kernel(x_prompt, x_sample, norm_mix_g, w_in, diff_q_norm_g, diff_k_norm_g, lam_q1, lam_k1, lam_q2, lam_k2,
           diff_subln_g, mla_q_latent_g, mla_w_uq, mla_kv_latent_g, mla_w_ukv, mla_q_norm_g, mla_k_norm_g,
           w_out, norm_ffn_g, peer_w_q, peer_key1, peer_key2, peer_u, peer_v):
    stacked = (norm_mix_g, w_in, diff_q_norm_g, diff_k_norm_g, lam_q1, lam_k1, lam_q2, lam_k2, diff_subln_g,
               mla_q_latent_g, mla_w_uq, mla_kv_latent_g, mla_w_ukv, mla_q_norm_g, mla_k_norm_g, w_out,
               norm_ffn_g, peer_w_q, peer_key1, peer_key2, peer_u, peer_v)
    y_prompt, y_sample = x_prompt, x_sample
    for l in range(DEPTH):
        p = _layer_params(l, *(w[l] for w in stacked))
        y_prompt = _layer(y_prompt, p)
        y_sample = _layer(y_sample, p)
    return (y_prompt, y_sample)
```

```python
import functools
import math

import jax
import jax.numpy as jnp
from jax import lax
from jax.experimental import pallas as pl
from jax.experimental.pallas import tpu as pltpu

F32 = jnp.float32
BF16 = jnp.bfloat16

D_MODEL = 1024
DEPTH = 2
N_DIFF_HEADS = 4
DIFF_QK_DIM = 64
DIFF_V_DIM = 128
N_MLA_HEADS = 4
MLA_Q_RANK = 256
MLA_KV_RANK = 256
MLA_NOPE_DIM = 64
MLA_ROPE_DIM = 32
MLA_QK_DIM = MLA_NOPE_DIM + MLA_ROPE_DIM
MLA_V_DIM = 128
ROPE_THETA = 10000.0
DIFF_Q_W = N_DIFF_HEADS * 2 * DIFF_QK_DIM
DIFF_V_W = N_DIFF_HEADS * DIFF_V_DIM
MLA_V_W = N_MLA_HEADS * MLA_V_DIM
IN_WIDTH = 3 * DIFF_Q_W + MLA_Q_RANK + MLA_KV_RANK + MLA_ROPE_DIM
PEER_HEADS = 8
PEER_N_KEYS = 128
PEER_N_EXPERTS = PEER_N_KEYS * PEER_N_KEYS
PEER_HALF = 128
PEER_TOPK = 16
EPS = 1e-6

LOG2E = 1.0 / math.log(2.0)
SCORE_BOUND_LOG2 = 60.0
BF16_ROUNDING_MARGIN = 1.02

KV_UNROLL = 2

HEAD_LANES = 128
V7X_VMEM_LIMIT = 56 * 1024 * 1024

_OFF_DQ = 0
_OFF_DK = DIFF_Q_W
_OFF_DV = 2 * DIFF_Q_W
_OFF_CQ = 3 * DIFF_Q_W
_OFF_CKV = _OFF_CQ + MLA_Q_RANK
_OFF_KR = _OFF_CKV + MLA_KV_RANK


def _nt_dot(a, b):
    return lax.dot_general(a, b, (((1,), (1,)), ((), ())), preferred_element_type=F32)


def _dot(a, b):
    return jnp.dot(a, b, preferred_element_type=F32)


def _group_rmsnorm_fm(a, n_groups, width, g_col):
    t = a.shape[1]
    a3 = a.reshape(n_groups, width, t)
    ms = jnp.mean(a3 * a3, axis=1, keepdims=True)
    return (a3 * lax.rsqrt(ms + EPS)).reshape(n_groups * width, t) * g_col


def _rope_fm(r, cos, sin):
    half = MLA_ROPE_DIM // 2
    r1, r2 = r[:half], r[half:]
    return r1 * cos - r2 * sin, r2 * cos + r1 * sin


def _pre_attn_kernel(x_ref, gmix_ref, wint_ref, gq_ref, gk_ref, gcq_ref, wuqt_ref, gckv_ref,
                     wukvt_ref, gmq_ref, gmk_ref, cos_ref, sin_ref,
                     qd_ref, kd_ref, vd_ref, qm_ref, km_ref, vm_ref, kt_scr):
    x = x_ref[0]
    t = x.shape[0]
    ms = jnp.mean(x * x, axis=-1, keepdims=True)
    xn = (x * lax.rsqrt(ms + EPS) * gmix_ref[...]).astype(BF16)
    ht = _nt_dot(wint_ref[...], xn)
    cos = cos_ref[...]
    sin = sin_ref[...]

    dq = _group_rmsnorm_fm(ht[_OFF_DQ:_OFF_DQ + DIFF_Q_W], 2 * N_DIFF_HEADS, DIFF_QK_DIM, gq_ref[...])
    qd_ref[0] = dq.astype(BF16)
    dk = _group_rmsnorm_fm(ht[_OFF_DK:_OFF_DK + DIFF_Q_W], 2 * N_DIFF_HEADS, DIFF_QK_DIM, gk_ref[...])
    kd_ref[0] = dk.T.astype(BF16)
    vd_ref[0] = ht[_OFF_DV:_OFF_DV + DIFF_V_W].astype(BF16)

    cq = ht[_OFF_CQ:_OFF_CQ + MLA_Q_RANK]
    cqn = (cq * lax.rsqrt(jnp.mean(cq * cq, axis=0, keepdims=True) + EPS) * gcq_ref[...]).astype(BF16)
    mq = _dot(wuqt_ref[...], cqn)
    pad = jnp.zeros((HEAD_LANES - MLA_QK_DIM, t), BF16)
    for h in range(N_MLA_HEADS):
        m = mq[h * MLA_QK_DIM:(h + 1) * MLA_QK_DIM]
        mn = m * lax.rsqrt(jnp.mean(m * m, axis=0, keepdims=True) + EPS) * gmq_ref[...]
        o1, o2 = _rope_fm(mn[MLA_NOPE_DIM:], cos, sin)
        base = h * HEAD_LANES
        qm_ref[0, base:base + MLA_NOPE_DIM] = mn[:MLA_NOPE_DIM].astype(BF16)
        qm_ref[0, base + MLA_NOPE_DIM:base + MLA_NOPE_DIM + 16] = o1.astype(BF16)
        qm_ref[0, base + MLA_NOPE_DIM + 16:base + MLA_QK_DIM] = o2.astype(BF16)
        qm_ref[0, base + MLA_QK_DIM:base + HEAD_LANES] = pad

    ckv = ht[_OFF_CKV:_OFF_CKV + MLA_KV_RANK]
    ckvn = (ckv * lax.rsqrt(jnp.mean(ckv * ckv, axis=0, keepdims=True) + EPS) * gckv_ref[...]).astype(BF16)
    kv = _dot(wukvt_ref[...], ckvn)
    kr = ht[_OFF_KR:_OFF_KR + MLA_ROPE_DIM]
    kr_ss = jnp.sum(kr * kr, axis=0, keepdims=True)
    gmk = gmk_ref[...]
    per_head = MLA_NOPE_DIM + MLA_V_DIM
    for h in range(N_MLA_HEADS):
        kn = kv[h * per_head:h * per_head + MLA_NOPE_DIM]
        v = kv[h * per_head + MLA_NOPE_DIM:(h + 1) * per_head]
        ms_k = (jnp.sum(kn * kn, axis=0, keepdims=True) + kr_ss) * (1.0 / MLA_QK_DIM)
        inv = lax.rsqrt(ms_k + EPS)
        o1, o2 = _rope_fm(kr * inv * gmk[MLA_NOPE_DIM:], cos, sin)
        base = h * HEAD_LANES
        kt_scr[base:base + MLA_NOPE_DIM] = kn * inv * gmk[:MLA_NOPE_DIM]
        kt_scr[base + MLA_NOPE_DIM:base + MLA_NOPE_DIM + 16] = o1
        kt_scr[base + MLA_NOPE_DIM + 16:base + MLA_QK_DIM] = o2
        kt_scr[base + MLA_QK_DIM:base + HEAD_LANES] = jnp.zeros((HEAD_LANES - MLA_QK_DIM, t), F32)
        vm_ref[0, base:base + HEAD_LANES] = v.astype(BF16)
    km_ref[0] = kt_scr[...].T.astype(BF16)


def _pre_attn(x, p, cos_t, sin_t, *, tile):
    b, s, d = x.shape
    hw = N_DIFF_HEADS * HEAD_LANES
    full = lambda a: pl.BlockSpec(a.shape, lambda bi, i: (0,) * a.ndim)
    fm_spec = pl.BlockSpec((1, hw, tile), lambda bi, i: (bi, 0, i))
    tm_spec = pl.BlockSpec((1, tile, hw), lambda bi, i: (bi, i, 0))
    weights = (p["gmix"], p["w_int"], p["gq"], p["gk"], p["gcq"], p["w_uqt"], p["gckv"], p["w_ukvt"],
               p["gmq"], p["gmk"])
    rope_spec = pl.BlockSpec((MLA_ROPE_DIM // 2, tile), lambda bi, i: (0, i))
    fm = jax.ShapeDtypeStruct((b, hw, s), BF16)
    tm = jax.ShapeDtypeStruct((b, s, hw), BF16)
    return pl.pallas_call(
        _pre_attn_kernel,
        grid=(b, s // tile),
        in_specs=[pl.BlockSpec((1, tile, d), lambda bi, i: (bi, i, 0))] + [full(w) for w in weights]
        + [rope_spec, rope_spec],
        out_specs=[fm_spec, tm_spec, fm_spec, fm_spec, tm_spec, fm_spec],
        out_shape=[fm, tm, fm, fm, tm, fm],
        scratch_shapes=[pltpu.VMEM((hw, tile), F32)],
        compiler_params=pltpu.CompilerParams(
            dimension_semantics=("parallel", "parallel"), vmem_limit_bytes=V7X_VMEM_LIMIT),
        name="pre_attn",
    )(x, *weights, cos_t, sin_t)


def _online_softmax_step(k, vt, qp, bias, m, l, acc_ref):
    s = _dot(k, qp)
    if bias is not None:
        s = s - bias
    mn = jnp.maximum(m, jnp.max(s, axis=0, keepdims=True))
    alpha = jnp.exp2(m - mn)
    p = jnp.exp2(s - mn)
    l = alpha * l + jnp.sum(p, axis=0, keepdims=True)
    acc_ref[...] = alpha * acc_ref[...] + _dot(vt, p.astype(BF16))
    return mn, l


def _bounded_softmax_step(k, vt, qp, bias, l, acc_ref):
    s = _dot(k, qp)
    if bias is not None:
        s = s - bias
    p = jnp.exp2(s)
    acc_ref[...] += _dot(vt, p.astype(BF16))
    return l + jnp.sum(p, axis=0, keepdims=True)


def _kv_chunk(k_ref, vt_ref, j, tk):
    off = pl.multiple_of(j * tk, tk)
    return k_ref[0, pl.ds(off, tk), :], vt_ref[0, :, pl.ds(off, tk)]


def _diff_attn_kernel(scal_ref, qt_ref, k_ref, vt_ref, g_ref, o_ref, acc0_ref, acc1_ref, *, tq, tk, n_kv):
    h = pl.program_id(1)
    i = pl.program_id(2)
    lam = scal_ref[0]
    sub_scale = scal_ref[1]
    bounded = scal_ref[2] > 0.5
    slope = scal_ref[3 + h]
    q = qt_ref[0]
    row = lax.broadcasted_iota(jnp.int32, q.shape, 0)
    zero = jnp.zeros_like(q)
    q0 = jnp.where(row < DIFF_QK_DIM, q, zero)
    q1 = jnp.where(row >= DIFF_QK_DIM, q, zero)
    d0 = (i * tq + lax.broadcasted_iota(jnp.int32, (tk, tq), 1)
          - lax.broadcasted_iota(jnp.int32, (tk, tq), 0)).astype(F32)
    acc0_ref[...] = jnp.zeros_like(acc0_ref)
    acc1_ref[...] = jnp.zeros_like(acc1_ref)

    neg = jnp.full((1, tq), -jnp.inf, F32)
    zer = jnp.zeros((1, tq), F32)

    def finish(l0, l1):
        o = acc0_ref[...] / l0 - lam * (acc1_ref[...] / l1)
        o = o * lax.rsqrt(jnp.mean(o * o, axis=0, keepdims=True) + EPS) * g_ref[...] * sub_scale
        o_ref[0] = o.T.astype(BF16)

    @pl.when(bounded)
    def _():
        def body(j, carry):
            l0, l1 = carry
            k, vt = _kv_chunk(k_ref, vt_ref, j, tk)
            bias = slope * jnp.abs(d0 - lax.convert_element_type(j * tk, F32))
            l0 = _bounded_softmax_step(k, vt, q0, bias, l0, acc0_ref)
            l1 = _bounded_softmax_step(k, vt, q1, bias, l1, acc1_ref)
            return l0, l1

        finish(*lax.fori_loop(0, n_kv, body, (zer, zer), unroll=KV_UNROLL))

    @pl.when(jnp.logical_not(bounded))
    def _():
        def body(j, carry):
            m0, l0, m1, l1 = carry
            k, vt = _kv_chunk(k_ref, vt_ref, j, tk)
            bias = slope * jnp.abs(d0 - lax.convert_element_type(j * tk, F32))
            m0, l0 = _online_softmax_step(k, vt, q0, bias, m0, l0, acc0_ref)
            m1, l1 = _online_softmax_step(k, vt, q1, bias, m1, l1, acc1_ref)
            return m0, l0, m1, l1

        _, l0, _, l1 = lax.fori_loop(0, n_kv, body, (neg, zer, neg, zer))
        finish(l0, l1)


def _mla_attn_kernel(scal_ref, qt_ref, k_ref, vt_ref, o_ref, acc_ref, *, tq, tk, n_kv):
    bounded = scal_ref[0] > 0.5
    q = qt_ref[0]
    acc_ref[...] = jnp.zeros_like(acc_ref)
    zer = jnp.zeros((1, tq), F32)

    def finish(l):
        o_ref[0] = (acc_ref[...] / l).T.astype(BF16)

    @pl.when(bounded)
    def _():
        def body(j, l):
            k, vt = _kv_chunk(k_ref, vt_ref, j, tk)
            return _bounded_softmax_step(k, vt, q, None, l, acc_ref)

        finish(lax.fori_loop(0, n_kv, body, zer, unroll=KV_UNROLL))

    @pl.when(jnp.logical_not(bounded))
    def _():
        def body(j, carry):
            k, vt = _kv_chunk(k_ref, vt_ref, j, tk)
            return _online_softmax_step(k, vt, q, None, *carry, acc_ref)

        _, l = lax.fori_loop(0, n_kv, body, (jnp.full((1, tq), -jnp.inf, F32), zer))
        finish(l)


def _attn_specs(s, tq):
    q_spec = pl.BlockSpec((1, HEAD_LANES, tq), lambda bi, h, i: (bi, h, i))
    k_spec = pl.BlockSpec((1, s, HEAD_LANES), lambda bi, h, i: (bi, 0, h))
    v_spec = pl.BlockSpec((1, HEAD_LANES, s), lambda bi, h, i: (bi, h, 0))
    o_spec = pl.BlockSpec((1, tq, HEAD_LANES), lambda bi, h, i: (bi, i, h))
    return q_spec, k_spec, v_spec, o_spec


def _diff_attn(scal, qt, k, vt, g_col, *, tq, tk):
    b, hw, s = qt.shape
    q_spec, k_spec, v_spec, o_spec = _attn_specs(s, tq)
    return pl.pallas_call(
        functools.partial(_diff_attn_kernel, tq=tq, tk=tk, n_kv=s // tk),
        grid=(b, N_DIFF_HEADS, s // tq),
        in_specs=[pl.BlockSpec(memory_space=pltpu.SMEM), q_spec, k_spec, v_spec,
                  pl.BlockSpec(g_col.shape, lambda bi, h, i: (0, 0))],
        out_specs=o_spec,
        out_shape=jax.ShapeDtypeStruct((b, s, hw), BF16),
        scratch_shapes=[pltpu.VMEM((HEAD_LANES, tq), F32), pltpu.VMEM((HEAD_LANES, tq), F32)],
        compiler_params=pltpu.CompilerParams(
            dimension_semantics=("parallel", "parallel", "parallel"), vmem_limit_bytes=V7X_VMEM_LIMIT),
        name="diff_attn",
    )(scal, qt, k, vt, g_col)


def _mla_attn(scal, qt, k, vt, *, tq, tk):
    b, hw, s = qt.shape
    q_spec, k_spec, v_spec, o_spec = _attn_specs(s, tq)
    return pl.pallas_call(
        functools.partial(_mla_attn_kernel, tq=tq, tk=tk, n_kv=s // tk),
        grid=(b, N_MLA_HEADS, s // tq),
        in_specs=[pl.BlockSpec(memory_space=pltpu.SMEM), q_spec, k_spec, v_spec],
        out_specs=o_spec,
        out_shape=jax.ShapeDtypeStruct((b, s, hw), BF16),
        scratch_shapes=[pltpu.VMEM((HEAD_LANES, tq), F32)],
        compiler_params=pltpu.CompilerParams(
            dimension_semantics=("parallel", "parallel", "parallel"), vmem_limit_bytes=V7X_VMEM_LIMIT),
        name="mla_attn",
    )(scal, qt, k, vt)


def _post_attn_kernel(x_ref, od_ref, om_ref, wout_ref, gffn_ref, wqt_ref, keys_ref,
                      x1_ref, xn_ref, sc_ref):
    x1 = (x_ref[0] + _dot(od_ref[0], wout_ref[:DIFF_V_W]) + _dot(om_ref[0], wout_ref[DIFF_V_W:]))
    x1_ref[0] = x1
    ms = jnp.mean(x1 * x1, axis=-1, keepdims=True)
    xn = (x1 * lax.rsqrt(ms + EPS) * gffn_ref[...]).astype(BF16)
    xn_ref[0] = xn
    qt = _nt_dot(wqt_ref[...], xn).astype(BF16)
    for hs in range(2 * PEER_HEADS):
        sc_ref[0, hs] = _dot(keys_ref[hs], qt[hs * PEER_HALF:(hs + 1) * PEER_HALF])


def _post_attn(x, od, om, p, *, tile):
    b, s, d = x.shape
    full = lambda a: pl.BlockSpec(a.shape, lambda bi, i: (0,) * a.ndim)
    tok = lambda w: pl.BlockSpec((1, tile, w), lambda bi, i: (bi, i, 0))
    weights = (p["w_out"], p["gffn"], p["w_qt"], p["keys"])
    return pl.pallas_call(
        _post_attn_kernel,
        grid=(b, s // tile),
        in_specs=[tok(d), tok(DIFF_V_W), tok(MLA_V_W)] + [full(w) for w in weights],
        out_specs=[tok(d), tok(d),
                   pl.BlockSpec((1, 2 * PEER_HEADS, PEER_N_KEYS, tile), lambda bi, i: (bi, 0, 0, i))],
        out_shape=[jax.ShapeDtypeStruct((b, s, d), F32), jax.ShapeDtypeStruct((b, s, d), BF16),
                   jax.ShapeDtypeStruct((b, 2 * PEER_HEADS, PEER_N_KEYS, s), F32)],
        compiler_params=pltpu.CompilerParams(
            dimension_semantics=("parallel", "parallel"), vmem_limit_bytes=V7X_VMEM_LIMIT),
        name="post_attn",
    )(x, od, om, *weights)


_SUB = 8
GELU_HALF = 0.5
LANES = 128
GATE_LANES = 256


def _top16_ranked(s, sub_idx, row16):
    rank = jnp.full(s.shape, float(PEER_N_KEYS - 1), F32)
    stacked = jnp.zeros(row16.shape, F32)
    vals = []
    for k in range(PEER_TOPK):
        m = jnp.max(s, axis=0, keepdims=True)
        first = jnp.min(jnp.where(s == m, sub_idx, float(PEER_N_KEYS)), axis=0, keepdims=True)
        sel = sub_idx == first
        rank = jnp.where(sel, float(k), rank)
        s = jnp.where(sel, -jnp.inf, s)
        stacked = jnp.where(row16 == float(k), m, stacked)
        vals.append(m)
    return vals, stacked, rank


_CODE_BASE = -(2.0 ** 126)
_CODE_STEP = 2.0 ** 121


def _top16_untied(s, row16):
    stacked = jnp.zeros(row16.shape, F32)
    vals = []
    for k in range(PEER_TOPK):
        m = jnp.max(s, axis=0, keepdims=True)
        s = jnp.where(s == m, _CODE_BASE - k * _CODE_STEP, s)
        stacked = jnp.where(row16 == float(k), m, stacked)
        vals.append(m)
    coded = s <= _CODE_BASE
    rank = jnp.where(coded, s * (-1.0 / _CODE_STEP) - 32.0, float(PEER_N_KEYS - 1))
    n_coded = jnp.sum(jnp.where(coded, 1.0, 0.0), axis=0, keepdims=True)
    return vals, stacked, rank, n_coded


def _dup_bf16(v):
    bits = lax.bitcast_convert_type(v.astype(BF16).astype(F32), jnp.uint32)
    return bits | (bits >> 16)


def _peer_topk_kernel(sc_ref, cnt_ref, e1_ref, r2_ref, e2_ref, *, td):
    sub_idx = lax.broadcasted_iota(jnp.int32, (PEER_N_KEYS, td), 0).astype(F32)
    row16 = lax.broadcasted_iota(jnp.int32, (PEER_TOPK, td), 0).astype(F32)
    row8 = lax.broadcasted_iota(jnp.int32, (_SUB, td), 0).astype(F32)
    neg_inf = jnp.full((_SUB, td), -jnp.inf, F32)
    k_top = float(PEER_TOPK)
    col_rows = [min(_SUB, PEER_TOPK // (b + 1)) for b in range(_SUB)]
    col_invalid = functools.reduce(jnp.add, [jnp.where(row8 < float(na), 0.0, 1.0) for na in col_rows])

    def make_head(exact_ties):
        def head(h):
            bad = jnp.zeros((1, td), F32)
            s1 = sc_ref[0, 2 * h]
            s2 = sc_ref[0, 2 * h + 1]
            if exact_ties:
                v1, v1s, rank1 = _top16_ranked(s1, sub_idx, row16)
                v2, v2s, rank2 = _top16_ranked(s2, sub_idx, row16)
            else:
                v1, v1s, rank1, n1 = _top16_untied(s1, row16)
                v2, v2s, rank2, n2 = _top16_untied(s2, row16)
                bad = jnp.maximum(bad, jnp.abs(n1 - k_top) + jnp.abs(n2 - k_top))

            cells = [jnp.where(row8 < float(na), v1s[:_SUB] + v2[b], neg_inf) if na < _SUB
                     else v1s[:_SUB] + v2[b] for b, na in enumerate(col_rows)]
            flats = [row8 * k_top + float(b) for b in range(_SUB)]
            cells.append(v1s[_SUB:] + v2[0]); flats.append((row8 + float(_SUB)) * k_top)
            cells.append(v1[0] + v2s[_SUB:]); flats.append(row8 + float(_SUB))
            big = float(PEER_TOPK * PEER_TOPK)
            top = None
            z = jnp.zeros((1, td), F32)
            for _ in range(PEER_TOPK):
                m = jnp.max(functools.reduce(jnp.maximum, cells), axis=0, keepdims=True)
                if exact_ties:
                    cand = [jnp.where(c == m, f, big) for c, f in zip(cells, flats)]
                    first = jnp.min(functools.reduce(jnp.minimum, cand), axis=0, keepdims=True)
                    cells = [jnp.where(f == first, -jnp.inf, c) for c, f in zip(cells, flats)]
                else:
                    cells = [jnp.where(c == m, -jnp.inf, c) for c in cells]
                if top is None:
                    top = m
                z = z + jnp.exp(m - top)
            knocked = [jnp.where(c == -jnp.inf, 1.0, 0.0) for c in cells]
            tail0 = jnp.sum(knocked[_SUB + 1], axis=0, keepdims=True)
            count_lo = functools.reduce(jnp.add, knocked[:_SUB]) - col_invalid + jnp.where(row8 == 0.0, tail0, 0.0)
            count_hi = knocked[_SUB]
            if not exact_ties:
                total = jnp.sum(count_lo + count_hi, axis=0, keepdims=True)
                bad = jnp.maximum(bad, jnp.abs(total - k_top))

            cnt = jnp.zeros((PEER_N_KEYS, td), F32)
            for a in range(PEER_TOPK):
                src = count_lo if a < _SUB else count_hi
                cnt = jnp.where(rank1 == float(a), src[a % _SUB:a % _SUB + 1], cnt)
            e1 = jnp.exp(s1 - v1[0])
            e2 = jnp.exp(s2 - v2[0]) * (GELU_HALF / z)
            for lt in range(td // LANES):
                tok = slice(lt * LANES, (lt + 1) * LANES)
                cnt_ref[0, lt, h] = _dup_bf16(cnt[:, tok])
                e1_ref[0, lt, h] = _dup_bf16(e1[:, tok])
            r2_ref[0, h] = rank2.astype(BF16)
            e2_ref[0, h] = e2.astype(BF16)
            return bad

        return head

    untied_head = make_head(False)
    exact_head = make_head(True)

    def head(h, carry):
        bad = untied_head(h)

        @pl.when(jnp.max(bad) > 0.0)
        def _():
            exact_head(h)

        return carry

    lax.fori_loop(0, PEER_HEADS, head, 0)


def _peer_topk(sc, *, tile):
    b, _, _, s = sc.shape
    row_spec = pl.BlockSpec((1, tile // LANES, PEER_HEADS, PEER_N_KEYS, LANES), lambda bi, i: (bi, i, 0, 0, 0))
    row_shape = jax.ShapeDtypeStruct((b, s // LANES, PEER_HEADS, PEER_N_KEYS, LANES), jnp.uint32)
    key_spec = pl.BlockSpec((1, PEER_HEADS, PEER_N_KEYS, tile), lambda bi, i: (bi, 0, 0, i))
    key_shape = jax.ShapeDtypeStruct((b, PEER_HEADS, PEER_N_KEYS, s), BF16)
    return pl.pallas_call(
        functools.partial(_peer_topk_kernel, td=tile),
        grid=(b, s // tile),
        in_specs=[pl.BlockSpec((1, 2 * PEER_HEADS, PEER_N_KEYS, tile), lambda bi, i: (bi, 0, 0, i))],
        out_specs=[row_spec, row_spec, key_spec, key_spec],
        out_shape=[row_shape, row_shape, key_shape, key_shape],
        compiler_params=pltpu.CompilerParams(
            dimension_semantics=("parallel", "parallel"), vmem_limit_bytes=V7X_VMEM_LIMIT),
        name="peer_topk",
    )(sc)


def _gelu_twice(x):
    return x + x * lax.erf(x * (1.0 / math.sqrt(2.0)))


def _packed_row(ref, lane_tiles, h, i1):
    def one(lt):
        row = jnp.broadcast_to(ref[0, lt, h, pl.ds(i1, 1), :], (PEER_N_KEYS // 2, LANES))
        return pltpu.bitcast(row, BF16)
    return jnp.concatenate([one(lt) for lt in lane_tiles], axis=1)


def _peer_ffn_kernel(xn_ref, u_ref, vt_ref, cnt_ref, e1_ref, r2_ref, e2_ref, x1_ref, o_ref,
                     acc_ref, gh_ref, *, rows_per_step):
    j = pl.program_id(2)

    @pl.when(j == 0)
    def _():
        acc_ref[...] = jnp.zeros_like(acc_ref)

    tt = xn_ref.shape[1]
    zero = jnp.zeros((PEER_N_KEYS, GATE_LANES), BF16)
    ht = _nt_dot(u_ref[...], xn_ref[0])
    act = _gelu_twice(ht.astype(BF16))
    for r in range(rows_per_step):
        i1 = j * rows_per_step + r
        keys = slice(r * PEER_N_KEYS, (r + 1) * PEER_N_KEYS)
        for c in range(tt // GATE_LANES):
            tok = slice(c * GATE_LANES, (c + 1) * GATE_LANES)
            lane_tiles = range(c * GATE_LANES // LANES, (c + 1) * GATE_LANES // LANES)
            gate = zero
            for h in range(PEER_HEADS):
                count = _packed_row(cnt_ref, lane_tiles, h, i1)
                e1 = _packed_row(e1_ref, lane_tiles, h, i1)
                gate = gate + jnp.where(r2_ref[0, h, :, tok] < count, e2_ref[0, h, :, tok], zero) * e1
            gh_ref[keys, tok] = gate * act[keys, tok]
    acc_ref[...] += _dot(vt_ref[...], gh_ref[...])

    @pl.when(j == pl.num_programs(2) - 1)
    def _():
        o_ref[0] = x1_ref[0] + acc_ref[...].T


def _peer_ffn(xn, x1, tables, u_bf, vt_bf, *, tile, experts_per_step):
    b, s, d = x1.shape
    cnt, e1, r2, e2 = tables
    rows = experts_per_step // PEER_N_KEYS
    tok = pl.BlockSpec((1, tile, d), lambda bi, i, j: (bi, i, 0))
    row_tab = pl.BlockSpec((1, tile // LANES, PEER_HEADS, PEER_N_KEYS, LANES), lambda bi, i, j: (bi, i, 0, 0, 0))
    key_tab = pl.BlockSpec((1, PEER_HEADS, PEER_N_KEYS, tile), lambda bi, i, j: (bi, 0, 0, i))
    return pl.pallas_call(
        functools.partial(_peer_ffn_kernel, rows_per_step=rows),
        grid=(b, s // tile, PEER_N_EXPERTS // experts_per_step),
        in_specs=[tok,
                  pl.BlockSpec((experts_per_step, d), lambda bi, i, j: (j, 0)),
                  pl.BlockSpec((d, experts_per_step), lambda bi, i, j: (0, j)),
                  row_tab, row_tab, key_tab, key_tab, tok],
        out_specs=tok,
        out_shape=jax.ShapeDtypeStruct((b, s, d), F32),
        scratch_shapes=[pltpu.VMEM((d, tile), F32), pltpu.VMEM((experts_per_step, tile), BF16)],
        compiler_params=pltpu.CompilerParams(
            dimension_semantics=("parallel", "parallel", "arbitrary"), vmem_limit_bytes=V7X_VMEM_LIMIT),
        name="peer_ffn",
    )(xn, u_bf, vt_bf, cnt, e1, r2, e2, x1)


def _col(v, reps=1, scale=1.0):
    return (jnp.tile(v.astype(F32), reps) * scale).reshape(-1, 1)


def _score_bound(scale_dim, gq, gk):
    bound = scale_dim * jnp.max(jnp.abs(gq.astype(F32))) * jnp.max(jnp.abs(gk.astype(F32))) * BF16_ROUNDING_MARGIN
    return (bound <= SCORE_BOUND_LOG2).astype(F32)


def _layer_params(l, norm_mix_g, w_in, diff_q_norm_g, diff_k_norm_g, lam_q1, lam_k1, lam_q2, lam_k2,
                  diff_subln_g, mla_q_latent_g, mla_w_uq, mla_kv_latent_g, mla_w_ukv, mla_q_norm_g,
                  mla_k_norm_g, w_out, norm_ffn_g, peer_w_q, peer_key1, peer_key2, peer_u, peer_v):
    lam_init = 0.8 - 0.6 * math.exp(-0.3 * l)
    lam = (jnp.exp(jnp.sum(lam_q1.astype(F32) * lam_k1.astype(F32)))
           - jnp.exp(jnp.sum(lam_q2.astype(F32) * lam_k2.astype(F32))) + lam_init)
    slopes = LOG2E * 2.0 ** (-8.0 * jnp.arange(1, N_DIFF_HEADS + 1, dtype=F32) / N_DIFF_HEADS)
    diff_scale = LOG2E * DIFF_QK_DIM ** -0.5
    mla_scale = LOG2E * MLA_QK_DIM ** -0.5
    diff_bound = _score_bound(diff_scale * DIFF_QK_DIM, diff_q_norm_g, diff_k_norm_g)
    mla_bound = _score_bound(mla_scale * MLA_QK_DIM, mla_q_norm_g, mla_k_norm_g)
    scal = jnp.concatenate([jnp.stack([lam, jnp.asarray(1.0 - lam_init, F32), diff_bound]), slopes]).astype(F32)
    keys = jnp.stack([peer_key1, peer_key2], axis=1).reshape(2 * PEER_HEADS, PEER_N_KEYS, PEER_HALF)
    return dict(
        gmix=norm_mix_g.reshape(1, -1), w_int=w_in.T.astype(BF16),
        gq=_col(diff_q_norm_g, 2 * N_DIFF_HEADS, diff_scale), gk=_col(diff_k_norm_g, 2 * N_DIFF_HEADS),
        gcq=_col(mla_q_latent_g), w_uqt=mla_w_uq.T.astype(BF16),
        gckv=_col(mla_kv_latent_g), w_ukvt=mla_w_ukv.T.astype(BF16),
        gmq=_col(mla_q_norm_g, 1, mla_scale), gmk=_col(mla_k_norm_g),
        scal=scal, mla_scal=mla_bound.reshape(1), gsub=_col(diff_subln_g),
        w_out=w_out.astype(BF16), gffn=norm_ffn_g.reshape(1, -1), w_qt=peer_w_q.T.astype(BF16),
        keys=keys.astype(BF16), u=peer_u.astype(BF16), vt=peer_v.T.astype(BF16),
    )


def _rope_tables(s):
    half = MLA_ROPE_DIM // 2
    inv = ROPE_THETA ** (-jnp.arange(half, dtype=F32) / half)
    ang = inv[:, None] * jnp.arange(s, dtype=F32)[None, :]
    return jnp.cos(ang), jnp.sin(ang)


def _tiles(s):
    return dict(proj=min(512, s), tq=min(512, s), tk_diff=min(1024, s), tk_mla=min(2048, s), topk=min(256, s), ffn=min(512, s))


def _layer(x, p):
    s = x.shape[1]
    t = _tiles(s)
    cos_t, sin_t = _rope_tables(s)
    qd, kd, vd, qm, km, vm = _pre_attn(x, p, cos_t, sin_t, tile=t["proj"])
    od = _diff_attn(p["scal"], qd, kd, vd, p["gsub"], tq=t["tq"], tk=t["tk_diff"])
    om = _mla_attn(p["mla_scal"], qm, km, vm, tq=t["tq"], tk=t["tk_mla"])
    x1, xn, sc = _post_attn(x, od, om, p, tile=t["proj"])
    tables = _peer_topk(sc, tile=t["topk"])
    return _peer_ffn(xn, x1, tables, p["u"], p["vt"], tile=t["ffn"], experts_per_step=2048)


def kernel(x_prompt, x_sample, norm_mix_g, w_in, diff_q_norm_g, diff_k_norm_g, lam_q1, lam_k1, lam_q2, lam_k2,
           diff_subln_g, mla_q_latent_g, mla_w_uq, mla_kv_latent_g, mla_w_ukv, mla_q_norm_g, mla_k_norm_g,
           w_out, norm_ffn_g, peer_w_q, peer_key1, peer_key2, peer_u, peer_v):
    stacked = (norm_mix_g, w_in, diff_q_norm_g, diff_k_norm_g, lam_q1, lam_k1, lam_q2, lam_k2, diff_subln_g,
               mla_q_latent_g, mla_w_uq, mla_kv_latent_g, mla_w_ukv, mla_q_norm_g, mla_k_norm_g, w_out,
               norm_ffn_g, peer_w_q, peer_key1, peer_key2, peer_u, peer_v)
    y_prompt, y_sample = x_prompt, x_sample
    for l in range(DEPTH):
        p = _layer_params(l, *(w[l] for w in stacked))
        y_prompt = _layer(y_prompt, p)
        y_sample = _layer(y_sample, p)
    return (y_prompt, y_sample)
```

```python
import functools
import math

import jax
import jax.numpy as jnp
from jax import lax
from jax.experimental import pallas as pl
from jax.experimental.pallas import tpu as pltpu

F32 = jnp.float32
BF16 = jnp.bfloat16

D_MODEL = 1024
DEPTH = 2
N_DIFF_HEADS = 4
DIFF_QK_DIM = 64
DIFF_V_DIM = 128
N_MLA_HEADS = 4
MLA_Q_RANK = 256
MLA_KV_RANK = 256
MLA_NOPE_DIM = 64
MLA_ROPE_DIM = 32
MLA_QK_DIM = MLA_NOPE_DIM + MLA_ROPE_DIM
MLA_V_DIM = 128
ROPE_THETA = 10000.0
DIFF_Q_W = N_DIFF_HEADS * 2 * DIFF_QK_DIM
DIFF_V_W = N_DIFF_HEADS * DIFF_V_DIM
MLA_V_W = N_MLA_HEADS * MLA_V_DIM
IN_WIDTH = 3 * DIFF_Q_W + MLA_Q_RANK + MLA_KV_RANK + MLA_ROPE_DIM
PEER_HEADS = 8
PEER_N_KEYS = 128
PEER_N_EXPERTS = PEER_N_KEYS * PEER_N_KEYS
PEER_HALF = 128
PEER_TOPK = 16
EPS = 1e-6

LOG2E = 1.0 / math.log(2.0)
SCORE_BOUND_LOG2 = 60.0
BF16_ROUNDING_MARGIN = 1.02

KV_UNROLL = 2

HEAD_LANES = 128
V7X_VMEM_LIMIT = 56 * 1024 * 1024

_OFF_DQ = 0
_OFF_DK = DIFF_Q_W
_OFF_DV = 2 * DIFF_Q_W
_OFF_CQ = 3 * DIFF_Q_W
_OFF_CKV = _OFF_CQ + MLA_Q_RANK
_OFF_KR = _OFF_CKV + MLA_KV_RANK


def _nt_dot(a, b):
    return lax.dot_general(a, b, (((1,), (1,)), ((), ())), preferred_element_type=F32)


def _dot(a, b):
    return jnp.dot(a, b, preferred_element_type=F32)


def _group_rmsnorm_fm(a, n_groups, width, g_col):
    t = a.shape[1]
    a3 = a.reshape(n_groups, width, t)
    ms = jnp.mean(a3 * a3, axis=1, keepdims=True)
    return (a3 * lax.rsqrt(ms + EPS)).reshape(n_groups * width, t) * g_col


def _rope_fm(r, cos, sin):
    half = MLA_ROPE_DIM // 2
    r1, r2 = r[:half], r[half:]
    return r1 * cos - r2 * sin, r2 * cos + r1 * sin


def _pre_attn_kernel(x_ref, gmix_ref, wint_ref, gq_ref, gk_ref, gcq_ref, wuqt_ref, gckv_ref,
                     wukvt_ref, gmq_ref, gmk_ref, cos_ref, sin_ref,
                     qd_ref, kd_ref, vd_ref, qm_ref, km_ref, vm_ref, kt_scr):
    x = x_ref[0]
    t = x.shape[0]
    ms = jnp.mean(x * x, axis=-1, keepdims=True)
    xn = (x * lax.rsqrt(ms + EPS) * gmix_ref[...]).astype(BF16)
    ht = _nt_dot(wint_ref[...], xn)
    cos = cos_ref[...]
    sin = sin_ref[...]

    dq = _group_rmsnorm_fm(ht[_OFF_DQ:_OFF_DQ + DIFF_Q_W], 2 * N_DIFF_HEADS, DIFF_QK_DIM, gq_ref[...])
    qd_ref[0] = dq.astype(BF16)
    dk = _group_rmsnorm_fm(ht[_OFF_DK:_OFF_DK + DIFF_Q_W], 2 * N_DIFF_HEADS, DIFF_QK_DIM, gk_ref[...])
    kd_ref[0] = dk.T.astype(BF16)
    vd_ref[0] = ht[_OFF_DV:_OFF_DV + DIFF_V_W].astype(BF16)

    cq = ht[_OFF_CQ:_OFF_CQ + MLA_Q_RANK]
    cqn = (cq * lax.rsqrt(jnp.mean(cq * cq, axis=0, keepdims=True) + EPS) * gcq_ref[...]).astype(BF16)
    mq = _dot(wuqt_ref[...], cqn)
    pad = jnp.zeros((HEAD_LANES - MLA_QK_DIM, t), BF16)
    for h in range(N_MLA_HEADS):
        m = mq[h * MLA_QK_DIM:(h + 1) * MLA_QK_DIM]
        mn = m * lax.rsqrt(jnp.mean(m * m, axis=0, keepdims=True) + EPS) * gmq_ref[...]
        o1, o2 = _rope_fm(mn[MLA_NOPE_DIM:], cos, sin)
        base = h * HEAD_LANES
        qm_ref[0, base:base + MLA_NOPE_DIM] = mn[:MLA_NOPE_DIM].astype(BF16)
        qm_ref[0, base + MLA_NOPE_DIM:base + MLA_NOPE_DIM + 16] = o1.astype(BF16)
        qm_ref[0, base + MLA_NOPE_DIM + 16:base + MLA_QK_DIM] = o2.astype(BF16)
        qm_ref[0, base + MLA_QK_DIM:base + HEAD_LANES] = pad

    ckv = ht[_OFF_CKV:_OFF_CKV + MLA_KV_RANK]
    ckvn = (ckv * lax.rsqrt(jnp.mean(ckv * ckv, axis=0, keepdims=True) + EPS) * gckv_ref[...]).astype(BF16)
    kv = _dot(wukvt_ref[...], ckvn)
    kr = ht[_OFF_KR:_OFF_KR + MLA_ROPE_DIM]
    kr_ss = jnp.sum(kr * kr, axis=0, keepdims=True)
    gmk = gmk_ref[...]
    per_head = MLA_NOPE_DIM + MLA_V_DIM
    for h in range(N_MLA_HEADS):
        kn = kv[h * per_head:h * per_head + MLA_NOPE_DIM]
        v = kv[h * per_head + MLA_NOPE_DIM:(h + 1) * per_head]
        ms_k = (jnp.sum(kn * kn, axis=0, keepdims=True) + kr_ss) * (1.0 / MLA_QK_DIM)
        inv = lax.rsqrt(ms_k + EPS)
        o1, o2 = _rope_fm(kr * inv * gmk[MLA_NOPE_DIM:], cos, sin)
        base = h * HEAD_LANES
        kt_scr[base:base + MLA_NOPE_DIM] = kn * inv * gmk[:MLA_NOPE_DIM]
        kt_scr[base + MLA_NOPE_DIM:base + MLA_NOPE_DIM + 16] = o1
        kt_scr[base + MLA_NOPE_DIM + 16:base + MLA_QK_DIM] = o2
        kt_scr[base + MLA_QK_DIM:base + HEAD_LANES] = jnp.zeros((HEAD_LANES - MLA_QK_DIM, t), F32)
        vm_ref[0, base:base + HEAD_LANES] = v.astype(BF16)
    km_ref[0] = kt_scr[...].T.astype(BF16)


def _pre_attn(x, p, cos_t, sin_t, *, tile):
    b, s, d = x.shape
    hw = N_DIFF_HEADS * HEAD_LANES
    full = lambda a: pl.BlockSpec(a.shape, lambda bi, i: (0,) * a.ndim)
    fm_spec = pl.BlockSpec((1, hw, tile), lambda bi, i: (bi, 0, i))
    tm_spec = pl.BlockSpec((1, tile, hw), lambda bi, i: (bi, i, 0))
    weights = (p["gmix"], p["w_int"], p["gq"], p["gk"], p["gcq"], p["w_uqt"], p["gckv"], p["w_ukvt"],
               p["gmq"], p["gmk"])
    rope_spec = pl.BlockSpec((MLA_ROPE_DIM // 2, tile), lambda bi, i: (0, i))
    fm = jax.ShapeDtypeStruct((b, hw, s), BF16)
    tm = jax.ShapeDtypeStruct((b, s, hw), BF16)
    return pl.pallas_call(
        _pre_attn_kernel,
        grid=(b, s // tile),
        in_specs=[pl.BlockSpec((1, tile, d), lambda bi, i: (bi, i, 0))] + [full(w) for w in weights]
        + [rope_spec, rope_spec],
        out_specs=[fm_spec, tm_spec, fm_spec, fm_spec, tm_spec, fm_spec],
        out_shape=[fm, tm, fm, fm, tm, fm],
        scratch_shapes=[pltpu.VMEM((hw, tile), F32)],
        compiler_params=pltpu.CompilerParams(
            dimension_semantics=("parallel", "parallel"), vmem_limit_bytes=V7X_VMEM_LIMIT),
        name="pre_attn",
    )(x, *weights, cos_t, sin_t)


def _online_softmax_step(k, vt, qp, bias, m, l, acc_ref):
    s = _dot(k, qp)
    if bias is not None:
        s = s - bias
    mn = jnp.maximum(m, jnp.max(s, axis=0, keepdims=True))
    alpha = jnp.exp2(m - mn)
    p = jnp.exp2(s - mn)
    l = alpha * l + jnp.sum(p, axis=0, keepdims=True)
    acc_ref[...] = alpha * acc_ref[...] + _dot(vt, p.astype(BF16))
    return mn, l


def _bounded_softmax_step(k, vt, qp, bias, l, acc_ref):
    s = _dot(k, qp)
    if bias is not None:
        s = s - bias
    p = jnp.exp2(s)
    acc_ref[...] += _dot(vt, p.astype(BF16))
    return l + jnp.sum(p, axis=0, keepdims=True)


def _kv_chunk(k_ref, vt_ref, j, tk):
    off = pl.multiple_of(j * tk, tk)
    return k_ref[0, pl.ds(off, tk), :], vt_ref[0, :, pl.ds(off, tk)]


def _diff_attn_kernel(scal_ref, qt_ref, k_ref, vt_ref, g_ref, o_ref, acc0_ref, acc1_ref, rel_ref, *, tq, tk, n_kv):
    h = pl.program_id(1)
    i = pl.program_id(2)
    lam = scal_ref[0]
    sub_scale = scal_ref[1]
    bounded = scal_ref[2] > 0.5
    slope = scal_ref[3 + h]
    q = qt_ref[0]
    row = lax.broadcasted_iota(jnp.int32, q.shape, 0)
    zero = jnp.zeros_like(q)
    q0 = jnp.where(row < DIFF_QK_DIM, q, zero)
    q1 = jnp.where(row >= DIFF_QK_DIM, q, zero)
    @pl.when(i == 0)
    def _():
        rel_ref[...] = (lax.broadcasted_iota(jnp.int32, (tk, tq), 1)
                        - lax.broadcasted_iota(jnp.int32, (tk, tq), 0)).astype(F32)

    def alibi_bias(j):
        shift = lax.convert_element_type(i * tq - j * tk, F32)
        return slope * jnp.abs(rel_ref[...] + shift)
    acc0_ref[...] = jnp.zeros_like(acc0_ref)
    acc1_ref[...] = jnp.zeros_like(acc1_ref)

    neg = jnp.full((1, tq), -jnp.inf, F32)
    zer = jnp.zeros((1, tq), F32)

    def finish(l0, l1):
        o = acc0_ref[...] / l0 - lam * (acc1_ref[...] / l1)
        o = o * lax.rsqrt(jnp.mean(o * o, axis=0, keepdims=True) + EPS) * g_ref[...] * sub_scale
        o_ref[0] = o.T.astype(BF16)

    @pl.when(bounded)
    def _():
        def body(j, carry):
            l0, l1 = carry
            k, vt = _kv_chunk(k_ref, vt_ref, j, tk)
            bias = alibi_bias(j)
            l0 = _bounded_softmax_step(k, vt, q0, bias, l0, acc0_ref)
            l1 = _bounded_softmax_step(k, vt, q1, bias, l1, acc1_ref)
            return l0, l1

        finish(*lax.fori_loop(0, n_kv, body, (zer, zer), unroll=KV_UNROLL))

    @pl.when(jnp.logical_not(bounded))
    def _():
        def body(j, carry):
            m0, l0, m1, l1 = carry
            k, vt = _kv_chunk(k_ref, vt_ref, j, tk)
            bias = alibi_bias(j)
            m0, l0 = _online_softmax_step(k, vt, q0, bias, m0, l0, acc0_ref)
            m1, l1 = _online_softmax_step(k, vt, q1, bias, m1, l1, acc1_ref)
            return m0, l0, m1, l1

        _, l0, _, l1 = lax.fori_loop(0, n_kv, body, (neg, zer, neg, zer))
        finish(l0, l1)


def _mla_attn_kernel(scal_ref, qt_ref, k_ref, vt_ref, o_ref, acc_ref, *, tq, tk, n_kv):
    bounded = scal_ref[0] > 0.5
    q = qt_ref[0]
    acc_ref[...] = jnp.zeros_like(acc_ref)
    zer = jnp.zeros((1, tq), F32)

    def finish(l):
        o_ref[0] = (acc_ref[...] / l).T.astype(BF16)

    @pl.when(bounded)
    def _():
        def body(j, l):
            k, vt = _kv_chunk(k_ref, vt_ref, j, tk)
            return _bounded_softmax_step(k, vt, q, None, l, acc_ref)

        finish(lax.fori_loop(0, n_kv, body, zer, unroll=KV_UNROLL))

    @pl.when(jnp.logical_not(bounded))
    def _():
        def body(j, carry):
            k, vt = _kv_chunk(k_ref, vt_ref, j, tk)
            return _online_softmax_step(k, vt, q, None, *carry, acc_ref)

        _, l = lax.fori_loop(0, n_kv, body, (jnp.full((1, tq), -jnp.inf, F32), zer))
        finish(l)


def _attn_specs(s, tq):
    q_spec = pl.BlockSpec((1, HEAD_LANES, tq), lambda bi, h, i: (bi, h, i))
    k_spec = pl.BlockSpec((1, s, HEAD_LANES), lambda bi, h, i: (bi, 0, h))
    v_spec = pl.BlockSpec((1, HEAD_LANES, s), lambda bi, h, i: (bi, h, 0))
    o_spec = pl.BlockSpec((1, tq, HEAD_LANES), lambda bi, h, i: (bi, i, h))
    return q_spec, k_spec, v_spec, o_spec


def _diff_attn(scal, qt, k, vt, g_col, *, tq, tk):
    b, hw, s = qt.shape
    q_spec, k_spec, v_spec, o_spec = _attn_specs(s, tq)
    return pl.pallas_call(
        functools.partial(_diff_attn_kernel, tq=tq, tk=tk, n_kv=s // tk),
        grid=(b, N_DIFF_HEADS, s // tq),
        in_specs=[pl.BlockSpec(memory_space=pltpu.SMEM), q_spec, k_spec, v_spec,
                  pl.BlockSpec(g_col.shape, lambda bi, h, i: (0, 0))],
        out_specs=o_spec,
        out_shape=jax.ShapeDtypeStruct((b, s, hw), BF16),
        scratch_shapes=[pltpu.VMEM((HEAD_LANES, tq), F32), pltpu.VMEM((HEAD_LANES, tq), F32),
                        pltpu.VMEM((tk, tq), F32)],
        compiler_params=pltpu.CompilerParams(
            dimension_semantics=("parallel", "parallel", "arbitrary"), vmem_limit_bytes=V7X_VMEM_LIMIT),
        name="diff_attn",
    )(scal, qt, k, vt, g_col)


def _mla_attn(scal, qt, k, vt, *, tq, tk):
    b, hw, s = qt.shape
    q_spec, k_spec, v_spec, o_spec = _attn_specs(s, tq)
    return pl.pallas_call(
        functools.partial(_mla_attn_kernel, tq=tq, tk=tk, n_kv=s // tk),
        grid=(b, N_MLA_HEADS, s // tq),
        in_specs=[pl.BlockSpec(memory_space=pltpu.SMEM), q_spec, k_spec, v_spec],
        out_specs=o_spec,
        out_shape=jax.ShapeDtypeStruct((b, s, hw), BF16),
        scratch_shapes=[pltpu.VMEM((HEAD_LANES, tq), F32)],
        compiler_params=pltpu.CompilerParams(
            dimension_semantics=("parallel", "parallel", "parallel"), vmem_limit_bytes=V7X_VMEM_LIMIT),
        name="mla_attn",
    )(scal, qt, k, vt)


def _post_attn_kernel(x_ref, od_ref, om_ref, wout_ref, gffn_ref, wqt_ref, keys_ref,
                      x1_ref, xn_ref, sc_ref):
    x1 = (x_ref[0] + _dot(od_ref[0], wout_ref[:DIFF_V_W]) + _dot(om_ref[0], wout_ref[DIFF_V_W:]))
    x1_ref[0] = x1
    ms = jnp.mean(x1 * x1, axis=-1, keepdims=True)
    xn = (x1 * lax.rsqrt(ms + EPS) * gffn_ref[...]).astype(BF16)
    xn_ref[0] = xn
    qt = _nt_dot(wqt_ref[...], xn).astype(BF16)
    for hs in range(2 * PEER_HEADS):
        sc_ref[0, hs] = _dot(keys_ref[hs], qt[hs * PEER_HALF:(hs + 1) * PEER_HALF])


def _post_attn(x, od, om, p, *, tile):
    b, s, d = x.shape
    full = lambda a: pl.BlockSpec(a.shape, lambda bi, i: (0,) * a.ndim)
    tok = lambda w: pl.BlockSpec((1, tile, w), lambda bi, i: (bi, i, 0))
    weights = (p["w_out"], p["gffn"], p["w_qt"], p["keys"])
    return pl.pallas_call(
        _post_attn_kernel,
        grid=(b, s // tile),
        in_specs=[tok(d), tok(DIFF_V_W), tok(MLA_V_W)] + [full(w) for w in weights],
        out_specs=[tok(d), tok(d),
                   pl.BlockSpec((1, 2 * PEER_HEADS, PEER_N_KEYS, tile), lambda bi, i: (bi, 0, 0, i))],
        out_shape=[jax.ShapeDtypeStruct((b, s, d), F32), jax.ShapeDtypeStruct((b, s, d), BF16),
                   jax.ShapeDtypeStruct((b, 2 * PEER_HEADS, PEER_N_KEYS, s), F32)],
        compiler_params=pltpu.CompilerParams(
            dimension_semantics=("parallel", "parallel"), vmem_limit_bytes=V7X_VMEM_LIMIT),
        name="post_attn",
    )(x, od, om, *weights)


_SUB = 8
GELU_HALF = 0.5
LANES = 128
GATE_LANES = 256


def _top16_ranked(s, sub_idx, row16):
    rank = jnp.full(s.shape, float(PEER_N_KEYS - 1), F32)
    stacked = jnp.zeros(row16.shape, F32)
    vals = []
    for k in range(PEER_TOPK):
        m = jnp.max(s, axis=0, keepdims=True)
        first = jnp.min(jnp.where(s == m, sub_idx, float(PEER_N_KEYS)), axis=0, keepdims=True)
        sel = sub_idx == first
        rank = jnp.where(sel, float(k), rank)
        s = jnp.where(sel, -jnp.inf, s)
        stacked = jnp.where(row16 == float(k), m, stacked)
        vals.append(m)
    return vals, stacked, rank


_CODE_BASE = -(2.0 ** 126)
_CODE_STEP = 2.0 ** 121


def _top16_untied(s, row16):
    stacked = jnp.zeros(row16.shape, F32)
    vals = []
    for k in range(PEER_TOPK):
        m = jnp.max(s, axis=0, keepdims=True)
        s = jnp.where(s == m, _CODE_BASE - k * _CODE_STEP, s)
        stacked = jnp.where(row16 == float(k), m, stacked)
        vals.append(m)
    coded = s <= _CODE_BASE
    rank = jnp.where(coded, s * (-1.0 / _CODE_STEP) - 32.0, float(PEER_N_KEYS - 1))
    n_coded = jnp.sum(jnp.where(coded, 1.0, 0.0), axis=0, keepdims=True)
    return vals, stacked, rank, n_coded


def _dup_bf16(v):
    bits = lax.bitcast_convert_type(v.astype(BF16).astype(F32), jnp.uint32)
    return bits | (bits >> 16)


def _peer_topk_kernel(sc_ref, cnt_ref, e1_ref, r2_ref, e2_ref, *, td):
    sub_idx = lax.broadcasted_iota(jnp.int32, (PEER_N_KEYS, td), 0).astype(F32)
    row16 = lax.broadcasted_iota(jnp.int32, (PEER_TOPK, td), 0).astype(F32)
    row8 = lax.broadcasted_iota(jnp.int32, (_SUB, td), 0).astype(F32)
    neg_inf = jnp.full((_SUB, td), -jnp.inf, F32)
    k_top = float(PEER_TOPK)
    col_rows = [min(_SUB, PEER_TOPK // (b + 1)) for b in range(_SUB)]
    col_invalid = functools.reduce(jnp.add, [jnp.where(row8 < float(na), 0.0, 1.0) for na in col_rows])

    def make_head(exact_ties):
        def head(h):
            bad = jnp.zeros((1, td), F32)
            s1 = sc_ref[0, 2 * h]
            s2 = sc_ref[0, 2 * h + 1]
            if exact_ties:
                v1, v1s, rank1 = _top16_ranked(s1, sub_idx, row16)
                v2, v2s, rank2 = _top16_ranked(s2, sub_idx, row16)
            else:
                v1, v1s, rank1, n1 = _top16_untied(s1, row16)
                v2, v2s, rank2, n2 = _top16_untied(s2, row16)
                bad = jnp.maximum(bad, jnp.abs(n1 - k_top) + jnp.abs(n2 - k_top))

            cells = [jnp.where(row8 < float(na), v1s[:_SUB] + v2[b], neg_inf) if na < _SUB
                     else v1s[:_SUB] + v2[b] for b, na in enumerate(col_rows)]
            flats = [row8 * k_top + float(b) for b in range(_SUB)]
            cells.append(v1s[_SUB:] + v2[0]); flats.append((row8 + float(_SUB)) * k_top)
            cells.append(v1[0] + v2s[_SUB:]); flats.append(row8 + float(_SUB))
            big = float(PEER_TOPK * PEER_TOPK)
            top = None
            z = jnp.zeros((1, td), F32)
            for _ in range(PEER_TOPK):
                m = jnp.max(functools.reduce(jnp.maximum, cells), axis=0, keepdims=True)
                if exact_ties:
                    cand = [jnp.where(c == m, f, big) for c, f in zip(cells, flats)]
                    first = jnp.min(functools.reduce(jnp.minimum, cand), axis=0, keepdims=True)
                    cells = [jnp.where(f == first, -jnp.inf, c) for c, f in zip(cells, flats)]
                else:
                    cells = [jnp.where(c == m, -jnp.inf, c) for c in cells]
                if top is None:
                    top = m
                z = z + jnp.exp(m - top)
            knocked = [jnp.where(c == -jnp.inf, 1.0, 0.0) for c in cells]
            tail0 = jnp.sum(knocked[_SUB + 1], axis=0, keepdims=True)
            count_lo = functools.reduce(jnp.add, knocked[:_SUB]) - col_invalid + jnp.where(row8 == 0.0, tail0, 0.0)
            count_hi = knocked[_SUB]
            if not exact_ties:
                total = jnp.sum(count_lo + count_hi, axis=0, keepdims=True)
                bad = jnp.maximum(bad, jnp.abs(total - k_top))

            cnt = jnp.zeros((PEER_N_KEYS, td), F32)
            for a in range(PEER_TOPK):
                src = count_lo if a < _SUB else count_hi
                cnt = jnp.where(rank1 == float(a), src[a % _SUB:a % _SUB + 1], cnt)
            e1 = jnp.exp(s1 - v1[0])
            e2 = jnp.exp(s2 - v2[0]) * (GELU_HALF / z)
            for lt in range(td // LANES):
                tok = slice(lt * LANES, (lt + 1) * LANES)
                cnt_ref[0, lt, h] = _dup_bf16(cnt[:, tok])
                e1_ref[0, lt, h] = _dup_bf16(e1[:, tok])
            r2_ref[0, h] = rank2.astype(BF16)
            e2_ref[0, h] = e2.astype(BF16)
            return bad

        return head

    untied_head = make_head(False)
    exact_head = make_head(True)

    def head(h, carry):
        bad = untied_head(h)

        @pl.when(jnp.max(bad) > 0.0)
        def _():
            exact_head(h)

        return carry

    lax.fori_loop(0, PEER_HEADS, head, 0)


def _peer_topk(sc, *, tile):
    b, _, _, s = sc.shape
    row_spec = pl.BlockSpec((1, tile // LANES, PEER_HEADS, PEER_N_KEYS, LANES), lambda bi, i: (bi, i, 0, 0, 0))
    row_shape = jax.ShapeDtypeStruct((b, s // LANES, PEER_HEADS, PEER_N_KEYS, LANES), jnp.uint32)
    key_spec = pl.BlockSpec((1, PEER_HEADS, PEER_N_KEYS, tile), lambda bi, i: (bi, 0, 0, i))
    key_shape = jax.ShapeDtypeStruct((b, PEER_HEADS, PEER_N_KEYS, s), BF16)
    return pl.pallas_call(
        functools.partial(_peer_topk_kernel, td=tile),
        grid=(b, s // tile),
        in_specs=[pl.BlockSpec((1, 2 * PEER_HEADS, PEER_N_KEYS, tile), lambda bi, i: (bi, 0, 0, i))],
        out_specs=[row_spec, row_spec, key_spec, key_spec],
        out_shape=[row_shape, row_shape, key_shape, key_shape],
        compiler_params=pltpu.CompilerParams(
            dimension_semantics=("parallel", "parallel"), vmem_limit_bytes=V7X_VMEM_LIMIT),
        name="peer_topk",
    )(sc)


def _gelu_twice(x):
    return x + x * lax.erf(x * (1.0 / math.sqrt(2.0)))


def _packed_row(ref, lane_tiles, h, i1):
    def one(lt):
        row = jnp.broadcast_to(ref[0, lt, h, pl.ds(i1, 1), :], (PEER_N_KEYS // 2, LANES))
        return pltpu.bitcast(row, BF16)
    return jnp.concatenate([one(lt) for lt in lane_tiles], axis=1)


def _peer_ffn_kernel(xn_ref, u_ref, vt_ref, cnt_ref, e1_ref, r2_ref, e2_ref, x1_ref, o_ref,
                     acc_ref, gh_ref, *, rows_per_step):
    j = pl.program_id(2)

    @pl.when(j == 0)
    def _():
        acc_ref[...] = jnp.zeros_like(acc_ref)

    tt = xn_ref.shape[1]
    zero = jnp.zeros((PEER_N_KEYS, GATE_LANES), BF16)
    ht = _nt_dot(u_ref[...], xn_ref[0])
    act = _gelu_twice(ht.astype(BF16))
    for r in range(rows_per_step):
        i1 = j * rows_per_step + r
        keys = slice(r * PEER_N_KEYS, (r + 1) * PEER_N_KEYS)
        for c in range(tt // GATE_LANES):
            tok = slice(c * GATE_LANES, (c + 1) * GATE_LANES)
            lane_tiles = range(c * GATE_LANES // LANES, (c + 1) * GATE_LANES // LANES)
            gate = zero
            for h in range(PEER_HEADS):
                count = _packed_row(cnt_ref, lane_tiles, h, i1)
                e1 = _packed_row(e1_ref, lane_tiles, h, i1)
                gate = gate + jnp.where(r2_ref[0, h, :, tok] < count, e2_ref[0, h, :, tok], zero) * e1
            gh_ref[keys, tok] = gate * act[keys, tok]
    acc_ref[...] += _dot(vt_ref[...], gh_ref[...])

    @pl.when(j == pl.num_programs(2) - 1)
    def _():
        o_ref[0] = x1_ref[0] + acc_ref[...].T


def _peer_ffn(xn, x1, tables, u_bf, vt_bf, *, tile, experts_per_step):
    b, s, d = x1.shape
    cnt, e1, r2, e2 = tables
    rows = experts_per_step // PEER_N_KEYS
    tok = pl.BlockSpec((1, tile, d), lambda bi, i, j: (bi, i, 0))
    row_tab = pl.BlockSpec((1, tile // LANES, PEER_HEADS, PEER_N_KEYS, LANES), lambda bi, i, j: (bi, i, 0, 0, 0))
    key_tab = pl.BlockSpec((1, PEER_HEADS, PEER_N_KEYS, tile), lambda bi, i, j: (bi, 0, 0, i))
    return pl.pallas_call(
        functools.partial(_peer_ffn_kernel, rows_per_step=rows),
        grid=(b, s // tile, PEER_N_EXPERTS // experts_per_step),
        in_specs=[tok,
                  pl.BlockSpec((experts_per_step, d), lambda bi, i, j: (j, 0)),
                  pl.BlockSpec((d, experts_per_step), lambda bi, i, j: (0, j)),
                  row_tab, row_tab, key_tab, key_tab, tok],
        out_specs=tok,
        out_shape=jax.ShapeDtypeStruct((b, s, d), F32),
        scratch_shapes=[pltpu.VMEM((d, tile), F32), pltpu.VMEM((experts_per_step, tile), BF16)],
        compiler_params=pltpu.CompilerParams(
            dimension_semantics=("parallel", "parallel", "arbitrary"), vmem_limit_bytes=V7X_VMEM_LIMIT),
        name="peer_ffn",
    )(xn, u_bf, vt_bf, cnt, e1, r2, e2, x1)


def _col(v, reps=1, scale=1.0):
    return (jnp.tile(v.astype(F32), reps) * scale).reshape(-1, 1)


def _score_bound(scale_dim, gq, gk):
    bound = scale_dim * jnp.max(jnp.abs(gq.astype(F32))) * jnp.max(jnp.abs(gk.astype(F32))) * BF16_ROUNDING_MARGIN
    return (bound <= SCORE_BOUND_LOG2).astype(F32)


def _layer_params(l, norm_mix_g, w_in, diff_q_norm_g, diff_k_norm_g, lam_q1, lam_k1, lam_q2, lam_k2,
                  diff_subln_g, mla_q_latent_g, mla_w_uq, mla_kv_latent_g, mla_w_ukv, mla_q_norm_g,
                  mla_k_norm_g, w_out, norm_ffn_g, peer_w_q, peer_key1, peer_key2, peer_u, peer_v):
    lam_init = 0.8 - 0.6 * math.exp(-0.3 * l)
    lam = (jnp.exp(jnp.sum(lam_q1.astype(F32) * lam_k1.astype(F32)))
           - jnp.exp(jnp.sum(lam_q2.astype(F32) * lam_k2.astype(F32))) + lam_init)
    slopes = LOG2E * 2.0 ** (-8.0 * jnp.arange(1, N_DIFF_HEADS + 1, dtype=F32) / N_DIFF_HEADS)
    diff_scale = LOG2E * DIFF_QK_DIM ** -0.5
    mla_scale = LOG2E * MLA_QK_DIM ** -0.5
    diff_bound = _score_bound(diff_scale * DIFF_QK_DIM, diff_q_norm_g, diff_k_norm_g)
    mla_bound = _score_bound(mla_scale * MLA_QK_DIM, mla_q_norm_g, mla_k_norm_g)
    scal = jnp.concatenate([jnp.stack([lam, jnp.asarray(1.0 - lam_init, F32), diff_bound]), slopes]).astype(F32)
    keys = jnp.stack([peer_key1, peer_key2], axis=1).reshape(2 * PEER_HEADS, PEER_N_KEYS, PEER_HALF)
    return dict(
        gmix=norm_mix_g.reshape(1, -1), w_int=w_in.T.astype(BF16),
        gq=_col(diff_q_norm_g, 2 * N_DIFF_HEADS, diff_scale), gk=_col(diff_k_norm_g, 2 * N_DIFF_HEADS),
        gcq=_col(mla_q_latent_g), w_uqt=mla_w_uq.T.astype(BF16),
        gckv=_col(mla_kv_latent_g), w_ukvt=mla_w_ukv.T.astype(BF16),
        gmq=_col(mla_q_norm_g, 1, mla_scale), gmk=_col(mla_k_norm_g),
        scal=scal, mla_scal=mla_bound.reshape(1), gsub=_col(diff_subln_g),
        w_out=w_out.astype(BF16), gffn=norm_ffn_g.reshape(1, -1), w_qt=peer_w_q.T.astype(BF16),
        keys=keys.astype(BF16), u=peer_u.astype(BF16), vt=peer_v.T.astype(BF16),
    )


def _rope_tables(s):
    half = MLA_ROPE_DIM // 2
    inv = ROPE_THETA ** (-jnp.arange(half, dtype=F32) / half)
    ang = inv[:, None] * jnp.arange(s, dtype=F32)[None, :]
    return jnp.cos(ang), jnp.sin(ang)


def _tiles(s):
    return dict(proj=min(512, s), tq=min(512, s), tk_diff=2048 if s >= 4096 else min(1024, s), tk_mla=min(2048, s), topk=min(256, s), ffn=min(512, s))


def _layer(x, p):
    s = x.shape[1]
    t = _tiles(s)
    cos_t, sin_t = _rope_tables(s)
    qd, kd, vd, qm, km, vm = _pre_attn(x, p, cos_t, sin_t, tile=t["proj"])
    od = _diff_attn(p["scal"], qd, kd, vd, p["gsub"], tq=t["tq"], tk=t["tk_diff"])
    om = _mla_attn(p["mla_scal"], qm, km, vm, tq=t["tq"], tk=t["tk_mla"])
    x1, xn, sc = _post_attn(x, od, om, p, tile=t["proj"])
    tables = _peer_topk(sc, tile=t["topk"])
    return _peer_ffn(xn, x1, tables, p["u"], p["vt"], tile=t["ffn"], experts_per_step=2048)


def kernel(x_prompt, x_sample, norm_mix_g, w_in, diff_q_norm_g, diff_k_norm_g, lam_q1, lam_k1, lam_q2, lam_k2,
           diff_subln_g, mla_q_latent_g, mla_w_uq, mla_kv_latent_g, mla_w_ukv, mla_q_norm_g, mla_k_norm_g,
           w_out, norm_ffn_g, peer_w_q, peer_key1, peer_key2, peer_u, peer_v):
    stacked = (norm_mix_g, w_in, diff_q_norm_g, diff_k_norm_g, lam_q1, lam_k1, lam_q2, lam_k2, diff_subln_g,
               mla_q_latent_g, mla_w_uq, mla_kv_latent_g, mla_w_ukv, mla_q_norm_g, mla_k_norm_g, w_out,
               norm_ffn_g, peer_w_q, peer_key1, peer_key2, peer_u, peer_v)
    y_prompt, y_sample = x_prompt, x_sample
    for l in range(DEPTH):
        p = _layer_params(l, *(w[l] for w in stacked))
        y_prompt = _layer(y_prompt, p)
        y_sample = _layer(y_sample, p)
    return (y_prompt, y_sample)
```

```python
import functools
import math

import jax
import jax.numpy as jnp
from jax import lax
from jax.experimental import pallas as pl
from jax.experimental.pallas import tpu as pltpu

F32 = jnp.float32
BF16 = jnp.bfloat16

D_MODEL = 1024
DEPTH = 2
N_DIFF_HEADS = 4
DIFF_QK_DIM = 64
DIFF_V_DIM = 128
N_MLA_HEADS = 4
MLA_Q_RANK = 256
MLA_KV_RANK = 256
MLA_NOPE_DIM = 64
MLA_ROPE_DIM = 32
MLA_QK_DIM = MLA_NOPE_DIM + MLA_ROPE_DIM
MLA_V_DIM = 128
ROPE_THETA = 10000.0
DIFF_Q_W = N_DIFF_HEADS * 2 * DIFF_QK_DIM
DIFF_V_W = N_DIFF_HEADS * DIFF_V_DIM
MLA_V_W = N_MLA_HEADS * MLA_V_DIM
IN_WIDTH = 3 * DIFF_Q_W + MLA_Q_RANK + MLA_KV_RANK + MLA_ROPE_DIM
PEER_HEADS = 8
PEER_N_KEYS = 128
PEER_N_EXPERTS = PEER_N_KEYS * PEER_N_KEYS
PEER_HALF = 128
PEER_TOPK = 16
EPS = 1e-6

LOG2E = 1.0 / math.log(2.0)
SCORE_BOUND_LOG2 = 60.0
BF16_ROUNDING_MARGIN = 1.02

KV_UNROLL = 2

HEAD_LANES = 128
V7X_VMEM_LIMIT = 56 * 1024 * 1024

_OFF_DQ = 0
_OFF_DK = DIFF_Q_W
_OFF_DV = 2 * DIFF_Q_W
_OFF_CQ = 3 * DIFF_Q_W
_OFF_CKV = _OFF_CQ + MLA_Q_RANK
_OFF_KR = _OFF_CKV + MLA_KV_RANK


def _nt_dot(a, b):
    return lax.dot_general(a, b, (((1,), (1,)), ((), ())), preferred_element_type=F32)


def _dot(a, b):
    return jnp.dot(a, b, preferred_element_type=F32)


def _group_rmsnorm_fm(a, n_groups, width, g_col):
    t = a.shape[1]
    a3 = a.reshape(n_groups, width, t)
    ms = jnp.mean(a3 * a3, axis=1, keepdims=True)
    return (a3 * lax.rsqrt(ms + EPS)).reshape(n_groups * width, t) * g_col


def _rope_fm(r, cos, sin):
    half = MLA_ROPE_DIM // 2
    r1, r2 = r[:half], r[half:]
    return r1 * cos - r2 * sin, r2 * cos + r1 * sin


def _pre_attn_kernel(x_ref, gmix_ref, wint_ref, gq_ref, gk_ref, gcq_ref, wuqt_ref, gckv_ref,
                     wukvt_ref, gmq_ref, gmk_ref, cos_ref, sin_ref,
                     qd_ref, kd_ref, vd_ref, qm_ref, km_ref, vm_ref, kt_scr):
    x = x_ref[0]
    t = x.shape[0]
    ms = jnp.mean(x * x, axis=-1, keepdims=True)
    xn = (x * lax.rsqrt(ms + EPS) * gmix_ref[...]).astype(BF16)
    ht = _nt_dot(wint_ref[...], xn)
    cos = cos_ref[...]
    sin = sin_ref[...]

    dq = _group_rmsnorm_fm(ht[_OFF_DQ:_OFF_DQ + DIFF_Q_W], 2 * N_DIFF_HEADS, DIFF_QK_DIM, gq_ref[...])
    qd_ref[0] = dq.astype(BF16)
    dk = _group_rmsnorm_fm(ht[_OFF_DK:_OFF_DK + DIFF_Q_W], 2 * N_DIFF_HEADS, DIFF_QK_DIM, gk_ref[...])
    kd_ref[0] = dk.T.astype(BF16)
    vd_ref[0] = ht[_OFF_DV:_OFF_DV + DIFF_V_W].astype(BF16)

    cq = ht[_OFF_CQ:_OFF_CQ + MLA_Q_RANK]
    cqn = (cq * lax.rsqrt(jnp.mean(cq * cq, axis=0, keepdims=True) + EPS) * gcq_ref[...]).astype(BF16)
    mq = _dot(wuqt_ref[...], cqn)
    pad = jnp.zeros((HEAD_LANES - MLA_QK_DIM, t), BF16)
    for h in range(N_MLA_HEADS):
        m = mq[h * MLA_QK_DIM:(h + 1) * MLA_QK_DIM]
        mn = m * lax.rsqrt(jnp.mean(m * m, axis=0, keepdims=True) + EPS) * gmq_ref[...]
        o1, o2 = _rope_fm(mn[MLA_NOPE_DIM:], cos, sin)
        base = h * HEAD_LANES
        qm_ref[0, base:base + MLA_NOPE_DIM] = mn[:MLA_NOPE_DIM].astype(BF16)
        qm_ref[0, base + MLA_NOPE_DIM:base + MLA_NOPE_DIM + 16] = o1.astype(BF16)
        qm_ref[0, base + MLA_NOPE_DIM + 16:base + MLA_QK_DIM] = o2.astype(BF16)
        qm_ref[0, base + MLA_QK_DIM:base + HEAD_LANES] = pad

    ckv = ht[_OFF_CKV:_OFF_CKV + MLA_KV_RANK]
    ckvn = (ckv * lax.rsqrt(jnp.mean(ckv * ckv, axis=0, keepdims=True) + EPS) * gckv_ref[...]).astype(BF16)
    kv = _dot(wukvt_ref[...], ckvn)
    kr = ht[_OFF_KR:_OFF_KR + MLA_ROPE_DIM]
    kr_ss = jnp.sum(kr * kr, axis=0, keepdims=True)
    gmk = gmk_ref[...]
    per_head = MLA_NOPE_DIM + MLA_V_DIM
    for h in range(N_MLA_HEADS):
        kn = kv[h * per_head:h * per_head + MLA_NOPE_DIM]
        v = kv[h * per_head + MLA_NOPE_DIM:(h + 1) * per_head]
        ms_k = (jnp.sum(kn * kn, axis=0, keepdims=True) + kr_ss) * (1.0 / MLA_QK_DIM)
        inv = lax.rsqrt(ms_k + EPS)
        o1, o2 = _rope_fm(kr * inv * gmk[MLA_NOPE_DIM:], cos, sin)
        base = h * HEAD_LANES
        kt_scr[base:base + MLA_NOPE_DIM] = kn * inv * gmk[:MLA_NOPE_DIM]
        kt_scr[base + MLA_NOPE_DIM:base + MLA_NOPE_DIM + 16] = o1
        kt_scr[base + MLA_NOPE_DIM + 16:base + MLA_QK_DIM] = o2
        kt_scr[base + MLA_QK_DIM:base + HEAD_LANES] = jnp.zeros((HEAD_LANES - MLA_QK_DIM, t), F32)
        vm_ref[0, base:base + HEAD_LANES] = v.astype(BF16)
    km_ref[0] = kt_scr[...].T.astype(BF16)


def _pre_attn(x, p, cos_t, sin_t, *, tile):
    b, s, d = x.shape
    hw = N_DIFF_HEADS * HEAD_LANES
    full = lambda a: pl.BlockSpec(a.shape, lambda bi, i: (0,) * a.ndim)
    fm_spec = pl.BlockSpec((1, hw, tile), lambda bi, i: (bi, 0, i))
    tm_spec = pl.BlockSpec((1, tile, hw), lambda bi, i: (bi, i, 0))
    weights = (p["gmix"], p["w_int"], p["gq"], p["gk"], p["gcq"], p["w_uqt"], p["gckv"], p["w_ukvt"],
               p["gmq"], p["gmk"])
    rope_spec = pl.BlockSpec((MLA_ROPE_DIM // 2, tile), lambda bi, i: (0, i))
    fm = jax.ShapeDtypeStruct((b, hw, s), BF16)
    tm = jax.ShapeDtypeStruct((b, s, hw), BF16)
    return pl.pallas_call(
        _pre_attn_kernel,
        grid=(b, s // tile),
        in_specs=[pl.BlockSpec((1, tile, d), lambda bi, i: (bi, i, 0))] + [full(w) for w in weights]
        + [rope_spec, rope_spec],
        out_specs=[fm_spec, tm_spec, fm_spec, fm_spec, tm_spec, fm_spec],
        out_shape=[fm, tm, fm, fm, tm, fm],
        scratch_shapes=[pltpu.VMEM((hw, tile), F32)],
        compiler_params=pltpu.CompilerParams(
            dimension_semantics=("parallel", "parallel"), vmem_limit_bytes=V7X_VMEM_LIMIT),
        name="pre_attn",
    )(x, *weights, cos_t, sin_t)


def _online_softmax_step(k, vt, qp, bias, m, l, acc_ref):
    s = _dot(k, qp)
    if bias is not None:
        s = s - bias
    mn = jnp.maximum(m, jnp.max(s, axis=0, keepdims=True))
    alpha = jnp.exp2(m - mn)
    p = jnp.exp2(s - mn)
    l = alpha * l + jnp.sum(p, axis=0, keepdims=True)
    acc_ref[...] = alpha * acc_ref[...] + _dot(vt, p.astype(BF16))
    return mn, l


def _bounded_softmax_step(k, vt, qp, bias, l, acc_ref):
    s = _dot(k, qp)
    if bias is not None:
        s = s - bias
    p = jnp.exp2(s)
    acc_ref[...] += _dot(vt, p.astype(BF16))
    return l + jnp.sum(p, axis=0, keepdims=True)


def _kv_chunk(k_ref, vt_ref, j, tk):
    off = pl.multiple_of(j * tk, tk)
    return k_ref[0, pl.ds(off, tk), :], vt_ref[0, :, pl.ds(off, tk)]


def _diff_attn_kernel(scal_ref, qt_ref, k_ref, vt_ref, g_ref, o_ref, acc0_ref, acc1_ref, rel_ref, *, tq, tk, n_kv):
    h = pl.program_id(1)
    i = pl.program_id(2)
    lam = scal_ref[0]
    sub_scale = scal_ref[1]
    bounded = scal_ref[2] > 0.5
    slope = scal_ref[3 + h]
    q = qt_ref[0]
    row = lax.broadcasted_iota(jnp.int32, q.shape, 0)
    zero = jnp.zeros_like(q)
    q0 = jnp.where(row < DIFF_QK_DIM, q, zero)
    q1 = jnp.where(row >= DIFF_QK_DIM, q, zero)
    @pl.when(i == 0)
    def _():
        rel_ref[...] = (lax.broadcasted_iota(jnp.int32, (tk, tq), 1)
                        - lax.broadcasted_iota(jnp.int32, (tk, tq), 0)).astype(F32)

    def alibi_bias(j):
        shift = lax.convert_element_type(i * tq - j * tk, F32)
        return slope * jnp.abs(rel_ref[...] + shift)
    acc0_ref[...] = jnp.zeros_like(acc0_ref)
    acc1_ref[...] = jnp.zeros_like(acc1_ref)

    neg = jnp.full((1, tq), -jnp.inf, F32)
    zer = jnp.zeros((1, tq), F32)

    def finish(l0, l1):
        o = acc0_ref[...] / l0 - lam * (acc1_ref[...] / l1)
        o = o * lax.rsqrt(jnp.mean(o * o, axis=0, keepdims=True) + EPS) * g_ref[...] * sub_scale
        o_ref[0] = o.T.astype(BF16)

    @pl.when(bounded)
    def _():
        def body(j, carry):
            l0, l1 = carry
            k, vt = _kv_chunk(k_ref, vt_ref, j, tk)
            bias = alibi_bias(j)
            l0 = _bounded_softmax_step(k, vt, q0, bias, l0, acc0_ref)
            l1 = _bounded_softmax_step(k, vt, q1, bias, l1, acc1_ref)
            return l0, l1

        finish(*lax.fori_loop(0, n_kv, body, (zer, zer), unroll=KV_UNROLL))

    @pl.when(jnp.logical_not(bounded))
    def _():
        def body(j, carry):
            m0, l0, m1, l1 = carry
            k, vt = _kv_chunk(k_ref, vt_ref, j, tk)
            bias = alibi_bias(j)
            m0, l0 = _online_softmax_step(k, vt, q0, bias, m0, l0, acc0_ref)
            m1, l1 = _online_softmax_step(k, vt, q1, bias, m1, l1, acc1_ref)
            return m0, l0, m1, l1

        _, l0, _, l1 = lax.fori_loop(0, n_kv, body, (neg, zer, neg, zer))
        finish(l0, l1)


def _mla_attn_kernel(scal_ref, qt_ref, k_ref, vt_ref, o_ref, acc_ref, *, tq, tk, n_kv):
    bounded = scal_ref[0] > 0.5
    q = qt_ref[0]
    acc_ref[...] = jnp.zeros_like(acc_ref)
    zer = jnp.zeros((1, tq), F32)

    def finish(l):
        o_ref[0] = (acc_ref[...] / l).T.astype(BF16)

    @pl.when(bounded)
    def _():
        def body(j, l):
            k, vt = _kv_chunk(k_ref, vt_ref, j, tk)
            return _bounded_softmax_step(k, vt, q, None, l, acc_ref)

        finish(lax.fori_loop(0, n_kv, body, zer, unroll=KV_UNROLL))

    @pl.when(jnp.logical_not(bounded))
    def _():
        def body(j, carry):
            k, vt = _kv_chunk(k_ref, vt_ref, j, tk)
            return _online_softmax_step(k, vt, q, None, *carry, acc_ref)

        _, l = lax.fori_loop(0, n_kv, body, (jnp.full((1, tq), -jnp.inf, F32), zer))
        finish(l)


def _attn_specs(s, tq):
    q_spec = pl.BlockSpec((1, HEAD_LANES, tq), lambda bi, h, i: (bi, h, i))
    k_spec = pl.BlockSpec((1, s, HEAD_LANES), lambda bi, h, i: (bi, 0, h))
    v_spec = pl.BlockSpec((1, HEAD_LANES, s), lambda bi, h, i: (bi, h, 0))
    o_spec = pl.BlockSpec((1, tq, HEAD_LANES), lambda bi, h, i: (bi, i, h))
    return q_spec, k_spec, v_spec, o_spec


def _diff_attn(scal, qt, k, vt, g_col, *, tq, tk):
    b, hw, s = qt.shape
    q_spec, k_spec, v_spec, o_spec = _attn_specs(s, tq)
    return pl.pallas_call(
        functools.partial(_diff_attn_kernel, tq=tq, tk=tk, n_kv=s // tk),
        grid=(b, N_DIFF_HEADS, s // tq),
        in_specs=[pl.BlockSpec(memory_space=pltpu.SMEM), q_spec, k_spec, v_spec,
                  pl.BlockSpec(g_col.shape, lambda bi, h, i: (0, 0))],
        out_specs=o_spec,
        out_shape=jax.ShapeDtypeStruct((b, s, hw), BF16),
        scratch_shapes=[pltpu.VMEM((HEAD_LANES, tq), F32), pltpu.VMEM((HEAD_LANES, tq), F32),
                        pltpu.VMEM((tk, tq), F32)],
        compiler_params=pltpu.CompilerParams(
            dimension_semantics=("parallel", "parallel", "arbitrary"), vmem_limit_bytes=V7X_VMEM_LIMIT),
        name="diff_attn",
    )(scal, qt, k, vt, g_col)


def _mla_attn(scal, qt, k, vt, *, tq, tk):
    b, hw, s = qt.shape
    q_spec, k_spec, v_spec, o_spec = _attn_specs(s, tq)
    return pl.pallas_call(
        functools.partial(_mla_attn_kernel, tq=tq, tk=tk, n_kv=s // tk),
        grid=(b, N_MLA_HEADS, s // tq),
        in_specs=[pl.BlockSpec(memory_space=pltpu.SMEM), q_spec, k_spec, v_spec],
        out_specs=o_spec,
        out_shape=jax.ShapeDtypeStruct((b, s, hw), BF16),
        scratch_shapes=[pltpu.VMEM((HEAD_LANES, tq), F32)],
        compiler_params=pltpu.CompilerParams(
            dimension_semantics=("parallel", "parallel", "parallel"), vmem_limit_bytes=V7X_VMEM_LIMIT),
        name="mla_attn",
    )(scal, qt, k, vt)


def _post_attn_kernel(x_ref, od_ref, om_ref, wout_ref, gffn_ref, wqt_ref, keys_ref,
                      x1_ref, xn_ref, sc_ref):
    x1 = (x_ref[0] + _dot(od_ref[0], wout_ref[:DIFF_V_W]) + _dot(om_ref[0], wout_ref[DIFF_V_W:]))
    x1_ref[0] = x1
    ms = jnp.mean(x1 * x1, axis=-1, keepdims=True)
    xn = (x1 * lax.rsqrt(ms + EPS) * gffn_ref[...]).astype(BF16)
    xn_ref[0] = xn
    qt = _nt_dot(wqt_ref[...], xn).astype(BF16)
    for hs in range(2 * PEER_HEADS):
        sc_ref[0, hs] = _dot(keys_ref[hs], qt[hs * PEER_HALF:(hs + 1) * PEER_HALF])


def _post_attn(x, od, om, p, *, tile):
    b, s, d = x.shape
    full = lambda a: pl.BlockSpec(a.shape, lambda bi, i: (0,) * a.ndim)
    tok = lambda w: pl.BlockSpec((1, tile, w), lambda bi, i: (bi, i, 0))
    weights = (p["w_out"], p["gffn"], p["w_qt"], p["keys"])
    return pl.pallas_call(
        _post_attn_kernel,
        grid=(b, s // tile),
        in_specs=[tok(d), tok(DIFF_V_W), tok(MLA_V_W)] + [full(w) for w in weights],
        out_specs=[tok(d), tok(d),
                   pl.BlockSpec((1, 2 * PEER_HEADS, PEER_N_KEYS, tile), lambda bi, i: (bi, 0, 0, i))],
        out_shape=[jax.ShapeDtypeStruct((b, s, d), F32), jax.ShapeDtypeStruct((b, s, d), BF16),
                   jax.ShapeDtypeStruct((b, 2 * PEER_HEADS, PEER_N_KEYS, s), F32)],
        compiler_params=pltpu.CompilerParams(
            dimension_semantics=("parallel", "parallel"), vmem_limit_bytes=V7X_VMEM_LIMIT),
        name="post_attn",
    )(x, od, om, *weights)


_SUB = 8
GELU_HALF = 0.5
LANES = 128
GATE_LANES = 256


def _top16_ranked(s, sub_idx, row16):
    rank = jnp.full(s.shape, float(PEER_N_KEYS - 1), F32)
    stacked = jnp.zeros(row16.shape, F32)
    vals = []
    for k in range(PEER_TOPK):
        m = jnp.max(s, axis=0, keepdims=True)
        first = jnp.min(jnp.where(s == m, sub_idx, float(PEER_N_KEYS)), axis=0, keepdims=True)
        sel = sub_idx == first
        rank = jnp.where(sel, float(k), rank)
        s = jnp.where(sel, -jnp.inf, s)
        stacked = jnp.where(row16 == float(k), m, stacked)
        vals.append(m)
    return vals, stacked, rank


_CODE_BASE = -(2.0 ** 126)
_CODE_STEP = 2.0 ** 121


def _top16_untied(s, row16):
    stacked = jnp.zeros(row16.shape, F32)
    vals = []
    for k in range(PEER_TOPK):
        m = jnp.max(s, axis=0, keepdims=True)
        s = jnp.where(s == m, _CODE_BASE - k * _CODE_STEP, s)
        stacked = jnp.where(row16 == float(k), m, stacked)
        vals.append(m)
    coded = s <= _CODE_BASE
    rank = jnp.where(coded, s * (-1.0 / _CODE_STEP) - 32.0, float(PEER_N_KEYS - 1))
    n_coded = jnp.sum(jnp.where(coded, 1.0, 0.0), axis=0, keepdims=True)
    return vals, stacked, rank, n_coded


def _peer_topk_kernel(sc_ref, cnt_ref, e1_ref, r2_ref, e2_ref, *, td):
    sub_idx = lax.broadcasted_iota(jnp.int32, (PEER_N_KEYS, td), 0).astype(F32)
    row16 = lax.broadcasted_iota(jnp.int32, (PEER_TOPK, td), 0).astype(F32)
    row8 = lax.broadcasted_iota(jnp.int32, (_SUB, td), 0).astype(F32)
    neg_inf = jnp.full((_SUB, td), -jnp.inf, F32)
    k_top = float(PEER_TOPK)
    col_rows = [min(_SUB, PEER_TOPK // (b + 1)) for b in range(_SUB)]
    col_invalid = functools.reduce(jnp.add, [jnp.where(row8 < float(na), 0.0, 1.0) for na in col_rows])

    def make_head(exact_ties):
        def head(h):
            bad = jnp.zeros((1, td), F32)
            s1 = sc_ref[0, 2 * h]
            s2 = sc_ref[0, 2 * h + 1]
            if exact_ties:
                v1, v1s, rank1 = _top16_ranked(s1, sub_idx, row16)
                v2, v2s, rank2 = _top16_ranked(s2, sub_idx, row16)
            else:
                v1, v1s, rank1, n1 = _top16_untied(s1, row16)
                v2, v2s, rank2, n2 = _top16_untied(s2, row16)
                bad = jnp.maximum(bad, jnp.abs(n1 - k_top) + jnp.abs(n2 - k_top))

            cells = [jnp.where(row8 < float(na), v1s[:_SUB] + v2[b], neg_inf) if na < _SUB
                     else v1s[:_SUB] + v2[b] for b, na in enumerate(col_rows)]
            flats = [row8 * k_top + float(b) for b in range(_SUB)]
            cells.append(v1s[_SUB:] + v2[0]); flats.append((row8 + float(_SUB)) * k_top)
            cells.append(v1[0] + v2s[_SUB:]); flats.append(row8 + float(_SUB))
            big = float(PEER_TOPK * PEER_TOPK)
            top = None
            z = jnp.zeros((1, td), F32)
            for _ in range(PEER_TOPK):
                m = jnp.max(functools.reduce(jnp.maximum, cells), axis=0, keepdims=True)
                if exact_ties:
                    cand = [jnp.where(c == m, f, big) for c, f in zip(cells, flats)]
                    first = jnp.min(functools.reduce(jnp.minimum, cand), axis=0, keepdims=True)
                    cells = [jnp.where(f == first, -jnp.inf, c) for c, f in zip(cells, flats)]
                else:
                    cells = [jnp.where(c == m, -jnp.inf, c) for c in cells]
                if top is None:
                    top = m
                z = z + jnp.exp(m - top)
            knocked = [jnp.where(c == -jnp.inf, 1.0, 0.0) for c in cells]
            tail0 = jnp.sum(knocked[_SUB + 1], axis=0, keepdims=True)
            count_lo = functools.reduce(jnp.add, knocked[:_SUB]) - col_invalid + jnp.where(row8 == 0.0, tail0, 0.0)
            count_hi = knocked[_SUB]
            if not exact_ties:
                total = jnp.sum(count_lo + count_hi, axis=0, keepdims=True)
                bad = jnp.maximum(bad, jnp.abs(total - k_top))

            cnt = jnp.zeros((PEER_N_KEYS, td), F32)
            for a in range(PEER_TOPK):
                src = count_lo if a < _SUB else count_hi
                cnt = jnp.where(rank1 == float(a), src[a % _SUB:a % _SUB + 1], cnt)
            e1 = jnp.exp(s1 - v1[0])
            e2 = jnp.exp(s2 - v2[0]) * (GELU_HALF / z)
            for lt in range(td // LANES):
                tok = slice(lt * LANES, (lt + 1) * LANES)
                cnt_ref[0, lt, h] = cnt[:, tok]
                e1_ref[0, lt, h] = e1[:, tok]
            r2_ref[0, h] = rank2.astype(BF16)
            e2_ref[0, h] = e2.astype(BF16)
            return bad

        return head

    untied_head = make_head(False)
    exact_head = make_head(True)

    def head(h, carry):
        bad = untied_head(h)

        @pl.when(jnp.max(bad) > 0.0)
        def _():
            exact_head(h)

        return carry

    lax.fori_loop(0, PEER_HEADS, head, 0)


def _peer_topk(sc, *, tile):
    b, _, _, s = sc.shape
    row_spec = pl.BlockSpec((1, tile // LANES, PEER_HEADS, PEER_N_KEYS, LANES), lambda bi, i: (bi, i, 0, 0, 0))
    row_shape = jax.ShapeDtypeStruct((b, s // LANES, PEER_HEADS, PEER_N_KEYS, LANES), F32)
    key_spec = pl.BlockSpec((1, PEER_HEADS, PEER_N_KEYS, tile), lambda bi, i: (bi, 0, 0, i))
    key_shape = jax.ShapeDtypeStruct((b, PEER_HEADS, PEER_N_KEYS, s), BF16)
    return pl.pallas_call(
        functools.partial(_peer_topk_kernel, td=tile),
        grid=(b, s // tile),
        in_specs=[pl.BlockSpec((1, 2 * PEER_HEADS, PEER_N_KEYS, tile), lambda bi, i: (bi, 0, 0, i))],
        out_specs=[row_spec, row_spec, key_spec, key_spec],
        out_shape=[row_shape, row_shape, key_shape, key_shape],
        compiler_params=pltpu.CompilerParams(
            dimension_semantics=("parallel", "parallel"), vmem_limit_bytes=V7X_VMEM_LIMIT),
        name="peer_topk",
    )(sc)


def _gelu_twice(x):
    return x + x * lax.erf(x * (1.0 / math.sqrt(2.0)))


def _packed_row(ref, lane_tiles, h, i1):
    rows = [jnp.broadcast_to(ref[0, lt, h, pl.ds(i1, 1), :], (PEER_N_KEYS, LANES)) for lt in lane_tiles]
    return jnp.concatenate(rows, axis=1).astype(BF16)


def _peer_ffn_kernel(xn_ref, u_ref, vt_ref, cnt_ref, e1_ref, r2_ref, e2_ref, x1_ref, o_ref,
                     acc_ref, gh_ref, *, rows_per_step):
    j = pl.program_id(2)

    @pl.when(j == 0)
    def _():
        acc_ref[...] = jnp.zeros_like(acc_ref)

    tt = xn_ref.shape[1]
    zero = jnp.zeros((PEER_N_KEYS, GATE_LANES), BF16)
    ht = _nt_dot(u_ref[...], xn_ref[0])
    act = _gelu_twice(ht.astype(BF16))
    for r in range(rows_per_step):
        i1 = j * rows_per_step + r
        keys = slice(r * PEER_N_KEYS, (r + 1) * PEER_N_KEYS)
        for c in range(tt // GATE_LANES):
            tok = slice(c * GATE_LANES, (c + 1) * GATE_LANES)
            lane_tiles = range(c * GATE_LANES // LANES, (c + 1) * GATE_LANES // LANES)
            gate = zero
            for h in range(PEER_HEADS):
                count = _packed_row(cnt_ref, lane_tiles, h, i1)
                e1 = _packed_row(e1_ref, lane_tiles, h, i1)
                gate = gate + jnp.where(r2_ref[0, h, :, tok] < count, e2_ref[0, h, :, tok], zero) * e1
            gh_ref[keys, tok] = gate * act[keys, tok]
    acc_ref[...] += _dot(vt_ref[...], gh_ref[...])

    @pl.when(j == pl.num_programs(2) - 1)
    def _():
        o_ref[0] = x1_ref[0] + acc_ref[...].T


def _peer_ffn(xn, x1, tables, u_bf, vt_bf, *, tile, experts_per_step):
    b, s, d = x1.shape
    cnt, e1, r2, e2 = tables
    rows = experts_per_step // PEER_N_KEYS
    tok = pl.BlockSpec((1, tile, d), lambda bi, i, j: (bi, i, 0))
    row_tab = pl.BlockSpec((1, tile // LANES, PEER_HEADS, PEER_N_KEYS, LANES), lambda bi, i, j: (bi, i, 0, 0, 0))
    key_tab = pl.BlockSpec((1, PEER_HEADS, PEER_N_KEYS, tile), lambda bi, i, j: (bi, 0, 0, i))
    return pl.pallas_call(
        functools.partial(_peer_ffn_kernel, rows_per_step=rows),
        grid=(b, s // tile, PEER_N_EXPERTS // experts_per_step),
        in_specs=[tok,
                  pl.BlockSpec((experts_per_step, d), lambda bi, i, j: (j, 0)),
                  pl.BlockSpec((d, experts_per_step), lambda bi, i, j: (0, j)),
                  row_tab, row_tab, key_tab, key_tab, tok],
        out_specs=tok,
        out_shape=jax.ShapeDtypeStruct((b, s, d), F32),
        scratch_shapes=[pltpu.VMEM((d, tile), F32), pltpu.VMEM((experts_per_step, tile), BF16)],
        compiler_params=pltpu.CompilerParams(
            dimension_semantics=("parallel", "parallel", "arbitrary"), vmem_limit_bytes=V7X_VMEM_LIMIT),
        name="peer_ffn",
    )(xn, u_bf, vt_bf, cnt, e1, r2, e2, x1)


def _col(v, reps=1, scale=1.0):
    return (jnp.tile(v.astype(F32), reps) * scale).reshape(-1, 1)


def _score_bound(scale_dim, gq, gk):
    bound = scale_dim * jnp.max(jnp.abs(gq.astype(F32))) * jnp.max(jnp.abs(gk.astype(F32))) * BF16_ROUNDING_MARGIN
    return (bound <= SCORE_BOUND_LOG2).astype(F32)


def _layer_params(l, norm_mix_g, w_in, diff_q_norm_g, diff_k_norm_g, lam_q1, lam_k1, lam_q2, lam_k2,
                  diff_subln_g, mla_q_latent_g, mla_w_uq, mla_kv_latent_g, mla_w_ukv, mla_q_norm_g,
                  mla_k_norm_g, w_out, norm_ffn_g, peer_w_q, peer_key1, peer_key2, peer_u, peer_v):
    lam_init = 0.8 - 0.6 * math.exp(-0.3 * l)
    lam = (jnp.exp(jnp.sum(lam_q1.astype(F32) * lam_k1.astype(F32)))
           - jnp.exp(jnp.sum(lam_q2.astype(F32) * lam_k2.astype(F32))) + lam_init)
    slopes = LOG2E * 2.0 ** (-8.0 * jnp.arange(1, N_DIFF_HEADS + 1, dtype=F32) / N_DIFF_HEADS)
    diff_scale = LOG2E * DIFF_QK_DIM ** -0.5
    mla_scale = LOG2E * MLA_QK_DIM ** -0.5
    diff_bound = _score_bound(diff_scale * DIFF_QK_DIM, diff_q_norm_g, diff_k_norm_g)
    mla_bound = _score_bound(mla_scale * MLA_QK_DIM, mla_q_norm_g, mla_k_norm_g)
    scal = jnp.concatenate([jnp.stack([lam, jnp.asarray(1.0 - lam_init, F32), diff_bound]), slopes]).astype(F32)
    keys = jnp.stack([peer_key1, peer_key2], axis=1).reshape(2 * PEER_HEADS, PEER_N_KEYS, PEER_HALF)
    return dict(
        gmix=norm_mix_g.reshape(1, -1), w_int=w_in.T.astype(BF16),
        gq=_col(diff_q_norm_g, 2 * N_DIFF_HEADS, diff_scale), gk=_col(diff_k_norm_g, 2 * N_DIFF_HEADS),
        gcq=_col(mla_q_latent_g), w_uqt=mla_w_uq.T.astype(BF16),
        gckv=_col(mla_kv_latent_g), w_ukvt=mla_w_ukv.T.astype(BF16),
        gmq=_col(mla_q_norm_g, 1, mla_scale), gmk=_col(mla_k_norm_g),
        scal=scal, mla_scal=mla_bound.reshape(1), gsub=_col(diff_subln_g),
        w_out=w_out.astype(BF16), gffn=norm_ffn_g.reshape(1, -1), w_qt=peer_w_q.T.astype(BF16),
        keys=keys.astype(BF16), u=peer_u.astype(BF16), vt=peer_v.T.astype(BF16),
    )


def _rope_tables(s):
    half = MLA_ROPE_DIM // 2
    inv = ROPE_THETA ** (-jnp.arange(half, dtype=F32) / half)
    ang = inv[:, None] * jnp.arange(s, dtype=F32)[None, :]
    return jnp.cos(ang), jnp.sin(ang)


def _tiles(s):
    return dict(proj=min(512, s), tq=min(512, s), tk_diff=2048 if s >= 4096 else min(1024, s), tk_mla=min(2048, s), topk=min(256, s), ffn=min(512, s))


def _layer(x, p):
    s = x.shape[1]
    t = _tiles(s)
    cos_t, sin_t = _rope_tables(s)
    qd, kd, vd, qm, km, vm = _pre_attn(x, p, cos_t, sin_t, tile=t["proj"])
    od = _diff_attn(p["scal"], qd, kd, vd, p["gsub"], tq=t["tq"], tk=t["tk_diff"])
    om = _mla_attn(p["mla_scal"], qm, km, vm, tq=t["tq"], tk=t["tk_mla"])
    x1, xn, sc = _post_attn(x, od, om, p, tile=t["proj"])
    tables = _peer_topk(sc, tile=t["topk"])
    return _peer_ffn(xn, x1, tables, p["u"], p["vt"], tile=t["ffn"], experts_per_step=2048)


def kernel(x_prompt, x_sample, norm_mix_g, w_in, diff_q_norm_g, diff_k_norm_g, lam_q1, lam_k1, lam_q2, lam_k2,
           diff_subln_g, mla_q_latent_g, mla_w_uq, mla_kv_latent_g, mla_w_ukv, mla_q_norm_g, mla_k_norm_g,
           w_out, norm_ffn_g, peer_w_q, peer_key1, peer_key2, peer_u, peer_v):
    stacked = (norm_mix_g, w_in, diff_q_norm_g, diff_k_norm_g, lam_q1, lam_k1, lam_q2, lam_k2, diff_subln_g,
               mla_q_latent_g, mla_w_uq, mla_kv_latent_g, mla_w_ukv, mla_q_norm_g, mla_k_norm_g, w_out,
               norm_ffn_g, peer_w_q, peer_key1, peer_key2, peer_u, peer_v)
    y_prompt, y_sample = x_prompt, x_sample
    for l in range(DEPTH):
        p = _layer_params(l, *(w[l] for w in stacked))
        y_prompt = _layer(y_prompt, p)
        y_sample = _layer(y_sample, p)
    return (y_prompt, y_sample)
```

```python
import functools
import math

import jax
import jax.numpy as jnp
from jax import lax
from jax.experimental import pallas as pl
from jax.experimental.pallas import tpu as pltpu

F32 = jnp.float32
BF16 = jnp.bfloat16

D_MODEL = 1024
DEPTH = 2
N_DIFF_HEADS = 4
DIFF_QK_DIM = 64
DIFF_V_DIM = 128
N_MLA_HEADS = 4
MLA_Q_RANK = 256
MLA_KV_RANK = 256
MLA_NOPE_DIM = 64
MLA_ROPE_DIM = 32
MLA_QK_DIM = MLA_NOPE_DIM + MLA_ROPE_DIM
MLA_V_DIM = 128
ROPE_THETA = 10000.0
DIFF_Q_W = N_DIFF_HEADS * 2 * DIFF_QK_DIM
DIFF_V_W = N_DIFF_HEADS * DIFF_V_DIM
MLA_V_W = N_MLA_HEADS * MLA_V_DIM
IN_WIDTH = 3 * DIFF_Q_W + MLA_Q_RANK + MLA_KV_RANK + MLA_ROPE_DIM
PEER_HEADS = 8
PEER_N_KEYS = 128
PEER_N_EXPERTS = PEER_N_KEYS * PEER_N_KEYS
PEER_HALF = 128
PEER_TOPK = 16
EPS = 1e-6

LOG2E = 1.0 / math.log(2.0)
SCORE_BOUND_LOG2 = 60.0
BF16_ROUNDING_MARGIN = 1.02

KV_UNROLL = 4

HEAD_LANES = 128
V7X_VMEM_LIMIT = 56 * 1024 * 1024

_OFF_DQ = 0
_OFF_DK = DIFF_Q_W
_OFF_DV = 2 * DIFF_Q_W
_OFF_CQ = 3 * DIFF_Q_W
_OFF_CKV = _OFF_CQ + MLA_Q_RANK
_OFF_KR = _OFF_CKV + MLA_KV_RANK


def _nt_dot(a, b):
    return lax.dot_general(a, b, (((1,), (1,)), ((), ())), preferred_element_type=F32)


def _dot(a, b):
    return jnp.dot(a, b, preferred_element_type=F32)


def _group_rmsnorm_fm(a, n_groups, width, g_col):
    t = a.shape[1]
    a3 = a.reshape(n_groups, width, t)
    ms = jnp.mean(a3 * a3, axis=1, keepdims=True)
    return (a3 * lax.rsqrt(ms + EPS)).reshape(n_groups * width, t) * g_col


def _rope_fm(r, cos, sin):
    half = MLA_ROPE_DIM // 2
    r1, r2 = r[:half], r[half:]
    return r1 * cos - r2 * sin, r2 * cos + r1 * sin


def _pre_attn_kernel(x_ref, gmix_ref, wint_ref, gq_ref, gk_ref, gcq_ref, wuqt_ref, gckv_ref,
                     wukvt_ref, gmq_ref, gmk_ref, cos_ref, sin_ref,
                     qd_ref, kd_ref, vd_ref, qm_ref, km_ref, vm_ref, kt_scr):
    x = x_ref[0]
    t = x.shape[0]
    ms = jnp.mean(x * x, axis=-1, keepdims=True)
    xn = (x * lax.rsqrt(ms + EPS) * gmix_ref[...]).astype(BF16)
    ht = _nt_dot(wint_ref[...], xn)
    cos = cos_ref[...]
    sin = sin_ref[...]

    dq = _group_rmsnorm_fm(ht[_OFF_DQ:_OFF_DQ + DIFF_Q_W], 2 * N_DIFF_HEADS, DIFF_QK_DIM, gq_ref[...])
    qd_ref[0] = dq.astype(BF16)
    dk = _group_rmsnorm_fm(ht[_OFF_DK:_OFF_DK + DIFF_Q_W], 2 * N_DIFF_HEADS, DIFF_QK_DIM, gk_ref[...])
    kd_ref[0] = dk.T.astype(BF16)
    vd_ref[0] = ht[_OFF_DV:_OFF_DV + DIFF_V_W].astype(BF16)

    cq = ht[_OFF_CQ:_OFF_CQ + MLA_Q_RANK]
    cqn = (cq * lax.rsqrt(jnp.mean(cq * cq, axis=0, keepdims=True) + EPS) * gcq_ref[...]).astype(BF16)
    mq = _dot(wuqt_ref[...], cqn)
    pad = jnp.zeros((HEAD_LANES - MLA_QK_DIM, t), BF16)
    for h in range(N_MLA_HEADS):
        m = mq[h * MLA_QK_DIM:(h + 1) * MLA_QK_DIM]
        mn = m * lax.rsqrt(jnp.mean(m * m, axis=0, keepdims=True) + EPS) * gmq_ref[...]
        o1, o2 = _rope_fm(mn[MLA_NOPE_DIM:], cos, sin)
        base = h * HEAD_LANES
        qm_ref[0, base:base + MLA_NOPE_DIM] = mn[:MLA_NOPE_DIM].astype(BF16)
        qm_ref[0, base + MLA_NOPE_DIM:base + MLA_NOPE_DIM + 16] = o1.astype(BF16)
        qm_ref[0, base + MLA_NOPE_DIM + 16:base + MLA_QK_DIM] = o2.astype(BF16)
        qm_ref[0, base + MLA_QK_DIM:base + HEAD_LANES] = pad

    ckv = ht[_OFF_CKV:_OFF_CKV + MLA_KV_RANK]
    ckvn = (ckv * lax.rsqrt(jnp.mean(ckv * ckv, axis=0, keepdims=True) + EPS) * gckv_ref[...]).astype(BF16)
    kv = _dot(wukvt_ref[...], ckvn)
    kr = ht[_OFF_KR:_OFF_KR + MLA_ROPE_DIM]
    kr_ss = jnp.sum(kr * kr, axis=0, keepdims=True)
    gmk = gmk_ref[...]
    per_head = MLA_NOPE_DIM + MLA_V_DIM
    for h in range(N_MLA_HEADS):
        kn = kv[h * per_head:h * per_head + MLA_NOPE_DIM]
        v = kv[h * per_head + MLA_NOPE_DIM:(h + 1) * per_head]
        ms_k = (jnp.sum(kn * kn, axis=0, keepdims=True) + kr_ss) * (1.0 / MLA_QK_DIM)
        inv = lax.rsqrt(ms_k + EPS)
        o1, o2 = _rope_fm(kr * inv * gmk[MLA_NOPE_DIM:], cos, sin)
        base = h * HEAD_LANES
        kt_scr[base:base + MLA_NOPE_DIM] = kn * inv * gmk[:MLA_NOPE_DIM]
        kt_scr[base + MLA_NOPE_DIM:base + MLA_NOPE_DIM + 16] = o1
        kt_scr[base + MLA_NOPE_DIM + 16:base + MLA_QK_DIM] = o2
        kt_scr[base + MLA_QK_DIM:base + HEAD_LANES] = jnp.zeros((HEAD_LANES - MLA_QK_DIM, t), F32)
        vm_ref[0, base:base + HEAD_LANES] = v.astype(BF16)
    km_ref[0] = kt_scr[...].T.astype(BF16)


def _pre_attn(x, p, cos_t, sin_t, *, tile):
    b, s, d = x.shape
    hw = N_DIFF_HEADS * HEAD_LANES
    full = lambda a: pl.BlockSpec(a.shape, lambda bi, i: (0,) * a.ndim)
    fm_spec = pl.BlockSpec((1, hw, tile), lambda bi, i: (bi, 0, i))
    tm_spec = pl.BlockSpec((1, tile, hw), lambda bi, i: (bi, i, 0))
    weights = (p["gmix"], p["w_int"], p["gq"], p["gk"], p["gcq"], p["w_uqt"], p["gckv"], p["w_ukvt"],
               p["gmq"], p["gmk"])
    rope_spec = pl.BlockSpec((MLA_ROPE_DIM // 2, tile), lambda bi, i: (0, i))
    fm = jax.ShapeDtypeStruct((b, hw, s), BF16)
    tm = jax.ShapeDtypeStruct((b, s, hw), BF16)
    return pl.pallas_call(
        _pre_attn_kernel,
        grid=(b, s // tile),
        in_specs=[pl.BlockSpec((1, tile, d), lambda bi, i: (bi, i, 0))] + [full(w) for w in weights]
        + [rope_spec, rope_spec],
        out_specs=[fm_spec, tm_spec, fm_spec, fm_spec, tm_spec, fm_spec],
        out_shape=[fm, tm, fm, fm, tm, fm],
        scratch_shapes=[pltpu.VMEM((hw, tile), F32)],
        compiler_params=pltpu.CompilerParams(
            dimension_semantics=("parallel", "parallel"), vmem_limit_bytes=V7X_VMEM_LIMIT),
        name="pre_attn",
    )(x, *weights, cos_t, sin_t)


def _online_softmax_step(k, vt, qp, bias, m, l, acc_ref):
    s = _dot(k, qp)
    if bias is not None:
        s = s - bias
    mn = jnp.maximum(m, jnp.max(s, axis=0, keepdims=True))
    alpha = jnp.exp2(m - mn)
    p = jnp.exp2(s - mn)
    l = alpha * l + jnp.sum(p, axis=0, keepdims=True)
    acc_ref[...] = alpha * acc_ref[...] + _dot(vt, p.astype(BF16))
    return mn, l


def _bounded_softmax_step(k, vt, qp, bias, l, acc_ref):
    s = _dot(k, qp)
    if bias is not None:
        s = s - bias
    p = jnp.exp2(s)
    acc_ref[...] += _dot(vt, p.astype(BF16))
    return l + jnp.sum(p, axis=0, keepdims=True)


def _kv_chunk(k_ref, vt_ref, j, tk):
    off = pl.multiple_of(j * tk, tk)
    return k_ref[0, pl.ds(off, tk), :], vt_ref[0, :, pl.ds(off, tk)]


def _diff_attn_kernel(scal_ref, qt_ref, k_ref, vt_ref, g_ref, o_ref, acc0_ref, acc1_ref, rel_ref, *, tq, tk, n_kv):
    h = pl.program_id(1)
    i = pl.program_id(2)
    lam = scal_ref[0]
    sub_scale = scal_ref[1]
    bounded = scal_ref[2] > 0.5
    slope = scal_ref[3 + h]
    q = qt_ref[0]
    row = lax.broadcasted_iota(jnp.int32, q.shape, 0)
    zero = jnp.zeros_like(q)
    q0 = jnp.where(row < DIFF_QK_DIM, q, zero)
    q1 = jnp.where(row >= DIFF_QK_DIM, q, zero)
    @pl.when(i == 0)
    def _():
        rel_ref[...] = (lax.broadcasted_iota(jnp.int32, (tk, tq), 1)
                        - lax.broadcasted_iota(jnp.int32, (tk, tq), 0)).astype(F32)

    def alibi_bias(j):
        shift = lax.convert_element_type(i * tq - j * tk, F32)
        return slope * jnp.abs(rel_ref[...] + shift)
    acc0_ref[...] = jnp.zeros_like(acc0_ref)
    acc1_ref[...] = jnp.zeros_like(acc1_ref)

    neg = jnp.full((1, tq), -jnp.inf, F32)
    zer = jnp.zeros((1, tq), F32)

    def finish(l0, l1):
        o = acc0_ref[...] / l0 - lam * (acc1_ref[...] / l1)
        o = o * lax.rsqrt(jnp.mean(o * o, axis=0, keepdims=True) + EPS) * g_ref[...] * sub_scale
        o_ref[0] = o.T.astype(BF16)

    @pl.when(bounded)
    def _():
        def body(j, carry):
            l0, l1 = carry
            k, vt = _kv_chunk(k_ref, vt_ref, j, tk)
            bias = alibi_bias(j)
            l0 = _bounded_softmax_step(k, vt, q0, bias, l0, acc0_ref)
            l1 = _bounded_softmax_step(k, vt, q1, bias, l1, acc1_ref)
            return l0, l1

        finish(*lax.fori_loop(0, n_kv, body, (zer, zer), unroll=KV_UNROLL))

    @pl.when(jnp.logical_not(bounded))
    def _():
        def body(j, carry):
            m0, l0, m1, l1 = carry
            k, vt = _kv_chunk(k_ref, vt_ref, j, tk)
            bias = alibi_bias(j)
            m0, l0 = _online_softmax_step(k, vt, q0, bias, m0, l0, acc0_ref)
            m1, l1 = _online_softmax_step(k, vt, q1, bias, m1, l1, acc1_ref)
            return m0, l0, m1, l1

        _, l0, _, l1 = lax.fori_loop(0, n_kv, body, (neg, zer, neg, zer))
        finish(l0, l1)


def _mla_attn_kernel(scal_ref, qt_ref, k_ref, vt_ref, o_ref, acc_ref, *, tq, tk, n_kv):
    bounded = scal_ref[0] > 0.5
    q = qt_ref[0]
    acc_ref[...] = jnp.zeros_like(acc_ref)
    zer = jnp.zeros((1, tq), F32)

    def finish(l):
        o_ref[0] = (acc_ref[...] / l).T.astype(BF16)

    @pl.when(bounded)
    def _():
        def body(j, l):
            k, vt = _kv_chunk(k_ref, vt_ref, j, tk)
            return _bounded_softmax_step(k, vt, q, None, l, acc_ref)

        finish(lax.fori_loop(0, n_kv, body, zer, unroll=KV_UNROLL))

    @pl.when(jnp.logical_not(bounded))
    def _():
        def body(j, carry):
            k, vt = _kv_chunk(k_ref, vt_ref, j, tk)
            return _online_softmax_step(k, vt, q, None, *carry, acc_ref)

        _, l = lax.fori_loop(0, n_kv, body, (jnp.full((1, tq), -jnp.inf, F32), zer))
        finish(l)


def _attn_specs(s, tq):
    q_spec = pl.BlockSpec((1, HEAD_LANES, tq), lambda bi, h, i: (bi, h, i))
    k_spec = pl.BlockSpec((1, s, HEAD_LANES), lambda bi, h, i: (bi, 0, h))
    v_spec = pl.BlockSpec((1, HEAD_LANES, s), lambda bi, h, i: (bi, h, 0))
    o_spec = pl.BlockSpec((1, tq, HEAD_LANES), lambda bi, h, i: (bi, i, h))
    return q_spec, k_spec, v_spec, o_spec


def _diff_attn(scal, qt, k, vt, g_col, *, tq, tk):
    b, hw, s = qt.shape
    q_spec, k_spec, v_spec, o_spec = _attn_specs(s, tq)
    return pl.pallas_call(
        functools.partial(_diff_attn_kernel, tq=tq, tk=tk, n_kv=s // tk),
        grid=(b, N_DIFF_HEADS, s // tq),
        in_specs=[pl.BlockSpec(memory_space=pltpu.SMEM), q_spec, k_spec, v_spec,
                  pl.BlockSpec(g_col.shape, lambda bi, h, i: (0, 0))],
        out_specs=o_spec,
        out_shape=jax.ShapeDtypeStruct((b, s, hw), BF16),
        scratch_shapes=[pltpu.VMEM((HEAD_LANES, tq), F32), pltpu.VMEM((HEAD_LANES, tq), F32),
                        pltpu.VMEM((tk, tq), F32)],
        compiler_params=pltpu.CompilerParams(
            dimension_semantics=("parallel", "parallel", "arbitrary"), vmem_limit_bytes=V7X_VMEM_LIMIT),
        name="diff_attn",
    )(scal, qt, k, vt, g_col)


def _mla_attn(scal, qt, k, vt, *, tq, tk):
    b, hw, s = qt.shape
    q_spec, k_spec, v_spec, o_spec = _attn_specs(s, tq)
    return pl.pallas_call(
        functools.partial(_mla_attn_kernel, tq=tq, tk=tk, n_kv=s // tk),
        grid=(b, N_MLA_HEADS, s // tq),
        in_specs=[pl.BlockSpec(memory_space=pltpu.SMEM), q_spec, k_spec, v_spec],
        out_specs=o_spec,
        out_shape=jax.ShapeDtypeStruct((b, s, hw), BF16),
        scratch_shapes=[pltpu.VMEM((HEAD_LANES, tq), F32)],
        compiler_params=pltpu.CompilerParams(
            dimension_semantics=("parallel", "parallel", "parallel"), vmem_limit_bytes=V7X_VMEM_LIMIT),
        name="mla_attn",
    )(scal, qt, k, vt)


def _post_attn_kernel(x_ref, od_ref, om_ref, wout_ref, gffn_ref, wqt_ref, keys_ref,
                      x1_ref, xn_ref, sc_ref):
    x1 = (x_ref[0] + _dot(od_ref[0], wout_ref[:DIFF_V_W]) + _dot(om_ref[0], wout_ref[DIFF_V_W:]))
    x1_ref[0] = x1
    ms = jnp.mean(x1 * x1, axis=-1, keepdims=True)
    xn = (x1 * lax.rsqrt(ms + EPS) * gffn_ref[...]).astype(BF16)
    xn_ref[0] = xn
    qt = _nt_dot(wqt_ref[...], xn).astype(BF16)
    for hs in range(2 * PEER_HEADS):
        sc_ref[0, hs] = _dot(keys_ref[hs], qt[hs * PEER_HALF:(hs + 1) * PEER_HALF])


def _post_attn(x, od, om, p, *, tile):
    b, s, d = x.shape
    full = lambda a: pl.BlockSpec(a.shape, lambda bi, i: (0,) * a.ndim)
    tok = lambda w: pl.BlockSpec((1, tile, w), lambda bi, i: (bi, i, 0))
    weights = (p["w_out"], p["gffn"], p["w_qt"], p["keys"])
    return pl.pallas_call(
        _post_attn_kernel,
        grid=(b, s // tile),
        in_specs=[tok(d), tok(DIFF_V_W), tok(MLA_V_W)] + [full(w) for w in weights],
        out_specs=[tok(d), tok(d),
                   pl.BlockSpec((1, 2 * PEER_HEADS, PEER_N_KEYS, tile), lambda bi, i: (bi, 0, 0, i))],
        out_shape=[jax.ShapeDtypeStruct((b, s, d), F32), jax.ShapeDtypeStruct((b, s, d), BF16),
                   jax.ShapeDtypeStruct((b, 2 * PEER_HEADS, PEER_N_KEYS, s), F32)],
        compiler_params=pltpu.CompilerParams(
            dimension_semantics=("parallel", "parallel"), vmem_limit_bytes=V7X_VMEM_LIMIT),
        name="post_attn",
    )(x, od, om, *weights)


_SUB = 8
GELU_HALF = 0.5
LANES = 128
GATE_LANES = 256


def _top16_ranked(s, sub_idx, row16):
    rank = jnp.full(s.shape, float(PEER_N_KEYS - 1), F32)
    stacked = jnp.zeros(row16.shape, F32)
    vals = []
    for k in range(PEER_TOPK):
        m = jnp.max(s, axis=0, keepdims=True)
        first = jnp.min(jnp.where(s == m, sub_idx, float(PEER_N_KEYS)), axis=0, keepdims=True)
        sel = sub_idx == first
        rank = jnp.where(sel, float(k), rank)
        s = jnp.where(sel, -jnp.inf, s)
        stacked = jnp.where(row16 == float(k), m, stacked)
        vals.append(m)
    return vals, stacked, rank


_CODE_BASE = -(2.0 ** 126)
_CODE_STEP = 2.0 ** 121


def _top16_untied(s, row16):
    stacked = jnp.zeros(row16.shape, F32)
    vals = []
    for k in range(PEER_TOPK):
        m = jnp.max(s, axis=0, keepdims=True)
        s = jnp.where(s == m, _CODE_BASE - k * _CODE_STEP, s)
        stacked = jnp.where(row16 == float(k), m, stacked)
        vals.append(m)
    coded = s <= _CODE_BASE
    rank = jnp.where(coded, s * (-1.0 / _CODE_STEP) - 32.0, float(PEER_N_KEYS - 1))
    n_coded = jnp.sum(jnp.where(coded, 1.0, 0.0), axis=0, keepdims=True)
    return vals, stacked, rank, n_coded


def _peer_topk_kernel(sc_ref, cnt_ref, e1_ref, r2_ref, e2_ref, *, td):
    sub_idx = lax.broadcasted_iota(jnp.int32, (PEER_N_KEYS, td), 0).astype(F32)
    row16 = lax.broadcasted_iota(jnp.int32, (PEER_TOPK, td), 0).astype(F32)
    row8 = lax.broadcasted_iota(jnp.int32, (_SUB, td), 0).astype(F32)
    neg_inf = jnp.full((_SUB, td), -jnp.inf, F32)
    k_top = float(PEER_TOPK)
    col_rows = [min(_SUB, PEER_TOPK // (b + 1)) for b in range(_SUB)]
    col_invalid = functools.reduce(jnp.add, [jnp.where(row8 < float(na), 0.0, 1.0) for na in col_rows])

    def make_head(exact_ties):
        def head(h):
            bad = jnp.zeros((1, td), F32)
            s1 = sc_ref[0, 2 * h]
            s2 = sc_ref[0, 2 * h + 1]
            if exact_ties:
                v1, v1s, rank1 = _top16_ranked(s1, sub_idx, row16)
                v2, v2s, rank2 = _top16_ranked(s2, sub_idx, row16)
            else:
                v1, v1s, rank1, n1 = _top16_untied(s1, row16)
                v2, v2s, rank2, n2 = _top16_untied(s2, row16)
                bad = jnp.maximum(bad, jnp.abs(n1 - k_top) + jnp.abs(n2 - k_top))

            cells = [jnp.where(row8 < float(na), v1s[:_SUB] + v2[b], neg_inf) if na < _SUB
                     else v1s[:_SUB] + v2[b] for b, na in enumerate(col_rows)]
            flats = [row8 * k_top + float(b) for b in range(_SUB)]
            cells.append(v1s[_SUB:] + v2[0]); flats.append((row8 + float(_SUB)) * k_top)
            cells.append(v1[0] + v2s[_SUB:]); flats.append(row8 + float(_SUB))
            big = float(PEER_TOPK * PEER_TOPK)
            top = None
            z = jnp.zeros((1, td), F32)
            for _ in range(PEER_TOPK):
                m = jnp.max(functools.reduce(jnp.maximum, cells), axis=0, keepdims=True)
                if exact_ties:
                    cand = [jnp.where(c == m, f, big) for c, f in zip(cells, flats)]
                    first = jnp.min(functools.reduce(jnp.minimum, cand), axis=0, keepdims=True)
                    cells = [jnp.where(f == first, -jnp.inf, c) for c, f in zip(cells, flats)]
                else:
                    cells = [jnp.where(c == m, -jnp.inf, c) for c in cells]
                if top is None:
                    top = m
                z = z + jnp.exp(m - top)
            knocked = [jnp.where(c == -jnp.inf, 1.0, 0.0) for c in cells]
            tail0 = jnp.sum(knocked[_SUB + 1], axis=0, keepdims=True)
            count_lo = functools.reduce(jnp.add, knocked[:_SUB]) - col_invalid + jnp.where(row8 == 0.0, tail0, 0.0)
            count_hi = knocked[_SUB]
            if not exact_ties:
                total = jnp.sum(count_lo + count_hi, axis=0, keepdims=True)
                bad = jnp.maximum(bad, jnp.abs(total - k_top))

            cnt = jnp.zeros((PEER_N_KEYS, td), F32)
            for a in range(PEER_TOPK):
                src = count_lo if a < _SUB else count_hi
                cnt = jnp.where(rank1 == float(a), src[a % _SUB:a % _SUB + 1], cnt)
            e1 = jnp.exp(s1 - v1[0])
            e2 = jnp.exp(s2 - v2[0]) * (GELU_HALF / z)
            for lt in range(td // LANES):
                tok = slice(lt * LANES, (lt + 1) * LANES)
                cnt_ref[0, lt, h] = cnt[:, tok]
                e1_ref[0, lt, h] = e1[:, tok]
            r2_ref[0, h] = rank2.astype(BF16)
            e2_ref[0, h] = e2.astype(BF16)
            return bad

        return head

    untied_head = make_head(False)
    exact_head = make_head(True)

    def head(h, carry):
        bad = untied_head(h)

        @pl.when(jnp.max(bad) > 0.0)
        def _():
            exact_head(h)

        return carry

    lax.fori_loop(0, PEER_HEADS, head, 0)


def _peer_topk(sc, *, tile):
    b, _, _, s = sc.shape
    row_spec = pl.BlockSpec((1, tile // LANES, PEER_HEADS, PEER_N_KEYS, LANES), lambda bi, i: (bi, i, 0, 0, 0))
    row_shape = jax.ShapeDtypeStruct((b, s // LANES, PEER_HEADS, PEER_N_KEYS, LANES), F32)
    key_spec = pl.BlockSpec((1, PEER_HEADS, PEER_N_KEYS, tile), lambda bi, i: (bi, 0, 0, i))
    key_shape = jax.ShapeDtypeStruct((b, PEER_HEADS, PEER_N_KEYS, s), BF16)
    return pl.pallas_call(
        functools.partial(_peer_topk_kernel, td=tile),
        grid=(b, s // tile),
        in_specs=[pl.BlockSpec((1, 2 * PEER_HEADS, PEER_N_KEYS, tile), lambda bi, i: (bi, 0, 0, i))],
        out_specs=[row_spec, row_spec, key_spec, key_spec],
        out_shape=[row_shape, row_shape, key_shape, key_shape],
        compiler_params=pltpu.CompilerParams(
            dimension_semantics=("parallel", "parallel"), vmem_limit_bytes=V7X_VMEM_LIMIT),
        name="peer_topk",
    )(sc)


def _gelu_twice(x):
    return x + x * lax.erf(x * (1.0 / math.sqrt(2.0)))


def _packed_row(ref, lane_tiles, h, i1):
    rows = [jnp.broadcast_to(ref[0, lt, h, pl.ds(i1, 1), :], (PEER_N_KEYS, LANES)) for lt in lane_tiles]
    return jnp.concatenate(rows, axis=1).astype(BF16)


def _peer_ffn_kernel(xn_ref, u_ref, vt_ref, cnt_ref, e1_ref, r2_ref, e2_ref, x1_ref, o_ref,
                     acc_ref, gh_ref, *, rows_per_step):
    j = pl.program_id(2)

    @pl.when(j == 0)
    def _():
        acc_ref[...] = jnp.zeros_like(acc_ref)

    tt = xn_ref.shape[1]
    zero = jnp.zeros((PEER_N_KEYS, GATE_LANES), BF16)
    ht = _nt_dot(u_ref[...], xn_ref[0])
    act = _gelu_twice(ht.astype(BF16))
    for r in range(rows_per_step):
        i1 = j * rows_per_step + r
        keys = slice(r * PEER_N_KEYS, (r + 1) * PEER_N_KEYS)
        for c in range(tt // GATE_LANES):
            tok = slice(c * GATE_LANES, (c + 1) * GATE_LANES)
            lane_tiles = range(c * GATE_LANES // LANES, (c + 1) * GATE_LANES // LANES)
            gate = zero
            for h in range(PEER_HEADS):
                count = _packed_row(cnt_ref, lane_tiles, h, i1)
                e1 = _packed_row(e1_ref, lane_tiles, h, i1)
                gate = gate + jnp.where(r2_ref[0, h, :, tok] < count, e2_ref[0, h, :, tok], zero) * e1
            gh_ref[keys, tok] = gate * act[keys, tok]
    acc_ref[...] += _dot(vt_ref[...], gh_ref[...])

    @pl.when(j == pl.num_programs(2) - 1)
    def _():
        o_ref[0] = x1_ref[0] + acc_ref[...].T


def _peer_ffn(xn, x1, tables, u_bf, vt_bf, *, tile, experts_per_step):
    b, s, d = x1.shape
    cnt, e1, r2, e2 = tables
    rows = experts_per_step // PEER_N_KEYS
    tok = pl.BlockSpec((1, tile, d), lambda bi, i, j: (bi, i, 0))
    row_tab = pl.BlockSpec((1, tile // LANES, PEER_HEADS, PEER_N_KEYS, LANES), lambda bi, i, j: (bi, i, 0, 0, 0))
    key_tab = pl.BlockSpec((1, PEER_HEADS, PEER_N_KEYS, tile), lambda bi, i, j: (bi, 0, 0, i))
    return pl.pallas_call(
        functools.partial(_peer_ffn_kernel, rows_per_step=rows),
        grid=(b, s // tile, PEER_N_EXPERTS // experts_per_step),
        in_specs=[tok,
                  pl.BlockSpec((experts_per_step, d), lambda bi, i, j: (j, 0)),
                  pl.BlockSpec((d, experts_per_step), lambda bi, i, j: (0, j)),
                  row_tab, row_tab, key_tab, key_tab, tok],
        out_specs=tok,
        out_shape=jax.ShapeDtypeStruct((b, s, d), F32),
        scratch_shapes=[pltpu.VMEM((d, tile), F32), pltpu.VMEM((experts_per_step, tile), BF16)],
        compiler_params=pltpu.CompilerParams(
            dimension_semantics=("parallel", "parallel", "arbitrary"), vmem_limit_bytes=V7X_VMEM_LIMIT),
        name="peer_ffn",
    )(xn, u_bf, vt_bf, cnt, e1, r2, e2, x1)


def _col(v, reps=1, scale=1.0):
    return (jnp.tile(v.astype(F32), reps) * scale).reshape(-1, 1)


def _score_bound(scale_dim, gq, gk):
    bound = scale_dim * jnp.max(jnp.abs(gq.astype(F32))) * jnp.max(jnp.abs(gk.astype(F32))) * BF16_ROUNDING_MARGIN
    return (bound <= SCORE_BOUND_LOG2).astype(F32)


def _layer_params(l, norm_mix_g, w_in, diff_q_norm_g, diff_k_norm_g, lam_q1, lam_k1, lam_q2, lam_k2,
                  diff_subln_g, mla_q_latent_g, mla_w_uq, mla_kv_latent_g, mla_w_ukv, mla_q_norm_g,
                  mla_k_norm_g, w_out, norm_ffn_g, peer_w_q, peer_key1, peer_key2, peer_u, peer_v):
    lam_init = 0.8 - 0.6 * math.exp(-0.3 * l)
    lam = (jnp.exp(jnp.sum(lam_q1.astype(F32) * lam_k1.astype(F32)))
           - jnp.exp(jnp.sum(lam_q2.astype(F32) * lam_k2.astype(F32))) + lam_init)
    slopes = LOG2E * 2.0 ** (-8.0 * jnp.arange(1, N_DIFF_HEADS + 1, dtype=F32) / N_DIFF_HEADS)
    diff_scale = LOG2E * DIFF_QK_DIM ** -0.5
    mla_scale = LOG2E * MLA_QK_DIM ** -0.5
    diff_bound = _score_bound(diff_scale * DIFF_QK_DIM, diff_q_norm_g, diff_k_norm_g)
    mla_bound = _score_bound(mla_scale * MLA_QK_DIM, mla_q_norm_g, mla_k_norm_g)
    scal = jnp.concatenate([jnp.stack([lam, jnp.asarray(1.0 - lam_init, F32), diff_bound]), slopes]).astype(F32)
    keys = jnp.stack([peer_key1, peer_key2], axis=1).reshape(2 * PEER_HEADS, PEER_N_KEYS, PEER_HALF)
    return dict(
        gmix=norm_mix_g.reshape(1, -1), w_int=w_in.T.astype(BF16),
        gq=_col(diff_q_norm_g, 2 * N_DIFF_HEADS, diff_scale), gk=_col(diff_k_norm_g, 2 * N_DIFF_HEADS),
        gcq=_col(mla_q_latent_g), w_uqt=mla_w_uq.T.astype(BF16),
        gckv=_col(mla_kv_latent_g), w_ukvt=mla_w_ukv.T.astype(BF16),
        gmq=_col(mla_q_norm_g, 1, mla_scale), gmk=_col(mla_k_norm_g),
        scal=scal, mla_scal=mla_bound.reshape(1), gsub=_col(diff_subln_g),
        w_out=w_out.astype(BF16), gffn=norm_ffn_g.reshape(1, -1), w_qt=peer_w_q.T.astype(BF16),
        keys=keys.astype(BF16), u=peer_u.astype(BF16), vt=peer_v.T.astype(BF16),
    )


def _rope_tables(s):
    half = MLA_ROPE_DIM // 2
    inv = ROPE_THETA ** (-jnp.arange(half, dtype=F32) / half)
    ang = inv[:, None] * jnp.arange(s, dtype=F32)[None, :]
    return jnp.cos(ang), jnp.sin(ang)


def _tiles(s):
    return dict(proj=min(512, s), tq=min(512, s), tk_diff=2048 if s >= 4096 else min(1024, s), tk_mla=min(2048, s), topk=min(256, s), ffn=min(512, s))


def _layer(x, p):
    s = x.shape[1]
    t = _tiles(s)
    cos_t, sin_t = _rope_tables(s)
    qd, kd, vd, qm, km, vm = _pre_attn(x, p, cos_t, sin_t, tile=t["proj"])
    od = _diff_attn(p["scal"], qd, kd, vd, p["gsub"], tq=t["tq"], tk=t["tk_diff"])
    om = _mla_attn(p["mla_scal"], qm, km, vm, tq=t["tq"], tk=t["tk_mla"])
    x1, xn, sc = _post_attn(x, od, om, p, tile=t["proj"])
    tables = _peer_topk(sc, tile=t["topk"])
    return _peer_ffn(xn, x1, tables, p["u"], p["vt"], tile=t["ffn"], experts_per_step=2048)


def kernel(x_prompt, x_sample, norm_mix_g, w_in, diff_q_norm_g, diff_k_norm_g, lam_q1, lam_k1, lam_q2, lam_k2,
           diff_subln_g, mla_q_latent_g, mla_w_uq, mla_kv_latent_g, mla_w_ukv, mla_q_norm_g, mla_k_norm_g,
           w_out, norm_ffn_g, peer_w_q, peer_key1, peer_key2, peer_u, peer_v):
    stacked = (norm_mix_g, w_in, diff_q_norm_g, diff_k_norm_g, lam_q1, lam_k1, lam_q2, lam_k2, diff_subln_g,
               mla_q_latent_g, mla_w_uq, mla_kv_latent_g, mla_w_ukv, mla_q_norm_g, mla_k_norm_g, w_out,
               norm_ffn_g, peer_w_q, peer_key1, peer_key2, peer_u, peer_v)
    y_prompt, y_sample = x_prompt, x_sample
    for l in range(DEPTH):
        p = _layer_params(l, *(w[l] for w in stacked))
        y_prompt = _layer(y_prompt, p)
        y_sample = _layer(y_sample, p)
    return (y_prompt, y_sample)
```

```python
import functools
import math

import jax
import jax.numpy as jnp
from jax import lax
from jax.experimental import pallas as pl
from jax.experimental.pallas import tpu as pltpu

F32 = jnp.float32
BF16 = jnp.bfloat16

D_MODEL = 1024
DEPTH = 2
N_DIFF_HEADS = 4
DIFF_QK_DIM = 64
DIFF_V_DIM = 128
N_MLA_HEADS = 4
MLA_Q_RANK = 256
MLA_KV_RANK = 256
MLA_NOPE_DIM = 64
MLA_ROPE_DIM = 32
MLA_QK_DIM = MLA_NOPE_DIM + MLA_ROPE_DIM
MLA_V_DIM = 128
ROPE_THETA = 10000.0
DIFF_Q_W = N_DIFF_HEADS * 2 * DIFF_QK_DIM
DIFF_V_W = N_DIFF_HEADS * DIFF_V_DIM
MLA_V_W = N_MLA_HEADS * MLA_V_DIM
IN_WIDTH = 3 * DIFF_Q_W + MLA_Q_RANK + MLA_KV_RANK + MLA_ROPE_DIM
PEER_HEADS = 8
PEER_N_KEYS = 128
PEER_N_EXPERTS = PEER_N_KEYS * PEER_N_KEYS
PEER_HALF = 128
PEER_TOPK = 16
EPS = 1e-6

LOG2E = 1.0 / math.log(2.0)
SCORE_BOUND_LOG2 = 60.0
BF16_ROUNDING_MARGIN = 1.02

KV_UNROLL = 4

HEAD_LANES = 128
V7X_VMEM_LIMIT = 56 * 1024 * 1024

_OFF_DQ = 0
_OFF_DK = DIFF_Q_W
_OFF_DV = 2 * DIFF_Q_W
_OFF_CQ = 3 * DIFF_Q_W
_OFF_CKV = _OFF_CQ + MLA_Q_RANK
_OFF_KR = _OFF_CKV + MLA_KV_RANK


def _nt_dot(a, b):
    return lax.dot_general(a, b, (((1,), (1,)), ((), ())), preferred_element_type=F32)


def _dot(a, b):
    return jnp.dot(a, b, preferred_element_type=F32)


def _group_rmsnorm_fm(a, n_groups, width, g_col):
    t = a.shape[1]
    a3 = a.reshape(n_groups, width, t)
    ms = jnp.mean(a3 * a3, axis=1, keepdims=True)
    return (a3 * lax.rsqrt(ms + EPS)).reshape(n_groups * width, t) * g_col


def _rope_fm(r, cos, sin):
    half = MLA_ROPE_DIM // 2
    r1, r2 = r[:half], r[half:]
    return r1 * cos - r2 * sin, r2 * cos + r1 * sin


def _pre_attn_kernel(x_ref, gmix_ref, wint_ref, gq_ref, gk_ref, gcq_ref, wuqt_ref, gckv_ref,
                     wukvt_ref, gmq_ref, gmk_ref, cos_ref, sin_ref,
                     qd_ref, kd_ref, vd_ref, qm_ref, km_ref, vm_ref, kt_scr):
    x = x_ref[0]
    t = x.shape[0]
    ms = jnp.mean(x * x, axis=-1, keepdims=True)
    xn = (x * lax.rsqrt(ms + EPS) * gmix_ref[...]).astype(BF16)
    ht = _nt_dot(wint_ref[...], xn)
    cos = cos_ref[...]
    sin = sin_ref[...]

    dq = _group_rmsnorm_fm(ht[_OFF_DQ:_OFF_DQ + DIFF_Q_W], 2 * N_DIFF_HEADS, DIFF_QK_DIM, gq_ref[...])
    qd_ref[0] = dq.astype(BF16)
    dk = _group_rmsnorm_fm(ht[_OFF_DK:_OFF_DK + DIFF_Q_W], 2 * N_DIFF_HEADS, DIFF_QK_DIM, gk_ref[...])
    kd_ref[0] = dk.T.astype(BF16)
    vd_ref[0] = ht[_OFF_DV:_OFF_DV + DIFF_V_W].astype(BF16)

    cq = ht[_OFF_CQ:_OFF_CQ + MLA_Q_RANK]
    cqn = (cq * lax.rsqrt(jnp.mean(cq * cq, axis=0, keepdims=True) + EPS) * gcq_ref[...]).astype(BF16)
    mq = _dot(wuqt_ref[...], cqn)
    pad = jnp.zeros((HEAD_LANES - MLA_QK_DIM, t), BF16)
    for h in range(N_MLA_HEADS):
        m = mq[h * MLA_QK_DIM:(h + 1) * MLA_QK_DIM]
        mn = m * lax.rsqrt(jnp.mean(m * m, axis=0, keepdims=True) + EPS) * gmq_ref[...]
        o1, o2 = _rope_fm(mn[MLA_NOPE_DIM:], cos, sin)
        base = h * HEAD_LANES
        qm_ref[0, base:base + MLA_NOPE_DIM] = mn[:MLA_NOPE_DIM].astype(BF16)
        qm_ref[0, base + MLA_NOPE_DIM:base + MLA_NOPE_DIM + 16] = o1.astype(BF16)
        qm_ref[0, base + MLA_NOPE_DIM + 16:base + MLA_QK_DIM] = o2.astype(BF16)
        qm_ref[0, base + MLA_QK_DIM:base + HEAD_LANES] = pad

    ckv = ht[_OFF_CKV:_OFF_CKV + MLA_KV_RANK]
    ckvn = (ckv * lax.rsqrt(jnp.mean(ckv * ckv, axis=0, keepdims=True) + EPS) * gckv_ref[...]).astype(BF16)
    kv = _dot(wukvt_ref[...], ckvn)
    kr = ht[_OFF_KR:_OFF_KR + MLA_ROPE_DIM]
    kr_ss = jnp.sum(kr * kr, axis=0, keepdims=True)
    gmk = gmk_ref[...]
    per_head = MLA_NOPE_DIM + MLA_V_DIM
    for h in range(N_MLA_HEADS):
        kn = kv[h * per_head:h * per_head + MLA_NOPE_DIM]
        v = kv[h * per_head + MLA_NOPE_DIM:(h + 1) * per_head]
        ms_k = (jnp.sum(kn * kn, axis=0, keepdims=True) + kr_ss) * (1.0 / MLA_QK_DIM)
        inv = lax.rsqrt(ms_k + EPS)
        o1, o2 = _rope_fm(kr * inv * gmk[MLA_NOPE_DIM:], cos, sin)
        base = h * HEAD_LANES
        kt_scr[base:base + MLA_NOPE_DIM] = kn * inv * gmk[:MLA_NOPE_DIM]
        kt_scr[base + MLA_NOPE_DIM:base + MLA_NOPE_DIM + 16] = o1
        kt_scr[base + MLA_NOPE_DIM + 16:base + MLA_QK_DIM] = o2
        kt_scr[base + MLA_QK_DIM:base + HEAD_LANES] = jnp.zeros((HEAD_LANES - MLA_QK_DIM, t), F32)
        vm_ref[0, base:base + HEAD_LANES] = v.astype(BF16)
    km_ref[0] = kt_scr[...].T.astype(BF16)


def _pre_attn(x, p, cos_t, sin_t, *, tile):
    b, s, d = x.shape
    hw = N_DIFF_HEADS * HEAD_LANES
    full = lambda a: pl.BlockSpec(a.shape, lambda bi, i: (0,) * a.ndim)
    fm_spec = pl.BlockSpec((1, hw, tile), lambda bi, i: (bi, 0, i))
    tm_spec = pl.BlockSpec((1, tile, hw), lambda bi, i: (bi, i, 0))
    weights = (p["gmix"], p["w_int"], p["gq"], p["gk"], p["gcq"], p["w_uqt"], p["gckv"], p["w_ukvt"],
               p["gmq"], p["gmk"])
    rope_spec = pl.BlockSpec((MLA_ROPE_DIM // 2, tile), lambda bi, i: (0, i))
    fm = jax.ShapeDtypeStruct((b, hw, s), BF16)
    tm = jax.ShapeDtypeStruct((b, s, hw), BF16)
    return pl.pallas_call(
        _pre_attn_kernel,
        grid=(b, s // tile),
        in_specs=[pl.BlockSpec((1, tile, d), lambda bi, i: (bi, i, 0))] + [full(w) for w in weights]
        + [rope_spec, rope_spec],
        out_specs=[fm_spec, tm_spec, fm_spec, fm_spec, tm_spec, fm_spec],
        out_shape=[fm, tm, fm, fm, tm, fm],
        scratch_shapes=[pltpu.VMEM((hw, tile), F32)],
        compiler_params=pltpu.CompilerParams(
            dimension_semantics=("parallel", "parallel"), vmem_limit_bytes=V7X_VMEM_LIMIT),
        name="pre_attn",
    )(x, *weights, cos_t, sin_t)


def _online_softmax_step(k, vt, qp, bias, m, l, acc_ref):
    s = _dot(k, qp)
    if bias is not None:
        s = s - bias
    mn = jnp.maximum(m, jnp.max(s, axis=0, keepdims=True))
    alpha = jnp.exp2(m - mn)
    p = jnp.exp2(s - mn)
    l = alpha * l + jnp.sum(p, axis=0, keepdims=True)
    acc_ref[...] = alpha * acc_ref[...] + _dot(vt, p.astype(BF16))
    return mn, l


def _bounded_softmax_step(k, vt, qp, bias, l, acc_ref):
    s = _dot(k, qp)
    if bias is not None:
        s = s - bias
    p = jnp.exp2(s)
    acc_ref[...] += _dot(vt, p.astype(BF16))
    return l + jnp.sum(p, axis=0, keepdims=True)


def _kv_chunk(k_ref, vt_ref, j, tk):
    off = pl.multiple_of(j * tk, tk)
    return k_ref[0, pl.ds(off, tk), :], vt_ref[0, :, pl.ds(off, tk)]


def _diff_attn_kernel(scal_ref, qt_ref, k_ref, vt_ref, g_ref, o_ref, acc0_ref, acc1_ref, rel_ref, *, tq, tk, n_kv):
    h = pl.program_id(1)
    i = pl.program_id(2)
    lam = scal_ref[0]
    sub_scale = scal_ref[1]
    bounded = scal_ref[2] > 0.5
    slope = scal_ref[3 + h]
    q = qt_ref[0]
    row = lax.broadcasted_iota(jnp.int32, q.shape, 0)
    zero = jnp.zeros_like(q)
    q0 = jnp.where(row < DIFF_QK_DIM, q, zero)
    q1 = jnp.where(row >= DIFF_QK_DIM, q, zero)
    @pl.when(i == 0)
    def _():
        rel_ref[...] = (lax.broadcasted_iota(jnp.int32, (tk, tq), 1)
                        - lax.broadcasted_iota(jnp.int32, (tk, tq), 0)).astype(F32)

    def alibi_bias(j):
        shift = lax.convert_element_type(i * tq - j * tk, F32)
        return slope * jnp.abs(rel_ref[...] + shift)
    acc0_ref[...] = jnp.zeros_like(acc0_ref)
    acc1_ref[...] = jnp.zeros_like(acc1_ref)

    neg = jnp.full((1, tq), -jnp.inf, F32)
    zer = jnp.zeros((1, tq), F32)

    def finish(l0, l1):
        o = acc0_ref[...] / l0 - lam * (acc1_ref[...] / l1)
        o = o * lax.rsqrt(jnp.mean(o * o, axis=0, keepdims=True) + EPS) * g_ref[...] * sub_scale
        o_ref[0] = o.T.astype(BF16)

    @pl.when(bounded)
    def _():
        def body(j, carry):
            l0, l1 = carry
            k, vt = _kv_chunk(k_ref, vt_ref, j, tk)
            bias = alibi_bias(j)
            l0 = _bounded_softmax_step(k, vt, q0, bias, l0, acc0_ref)
            l1 = _bounded_softmax_step(k, vt, q1, bias, l1, acc1_ref)
            return l0, l1

        finish(*lax.fori_loop(0, n_kv, body, (zer, zer), unroll=KV_UNROLL))

    @pl.when(jnp.logical_not(bounded))
    def _():
        def body(j, carry):
            m0, l0, m1, l1 = carry
            k, vt = _kv_chunk(k_ref, vt_ref, j, tk)
            bias = alibi_bias(j)
            m0, l0 = _online_softmax_step(k, vt, q0, bias, m0, l0, acc0_ref)
            m1, l1 = _online_softmax_step(k, vt, q1, bias, m1, l1, acc1_ref)
            return m0, l0, m1, l1

        _, l0, _, l1 = lax.fori_loop(0, n_kv, body, (neg, zer, neg, zer))
        finish(l0, l1)


def _mla_attn_kernel(scal_ref, qt_ref, k_ref, vt_ref, o_ref, acc_ref, *, tq, tk, n_kv):
    bounded = scal_ref[0] > 0.5
    q = qt_ref[0]
    acc_ref[...] = jnp.zeros_like(acc_ref)
    zer = jnp.zeros((1, tq), F32)

    def finish(l):
        o_ref[0] = (acc_ref[...] / l).T.astype(BF16)

    @pl.when(bounded)
    def _():
        def body(j, l):
            k, vt = _kv_chunk(k_ref, vt_ref, j, tk)
            return _bounded_softmax_step(k, vt, q, None, l, acc_ref)

        finish(lax.fori_loop(0, n_kv, body, zer, unroll=KV_UNROLL))

    @pl.when(jnp.logical_not(bounded))
    def _():
        def body(j, carry):
            k, vt = _kv_chunk(k_ref, vt_ref, j, tk)
            return _online_softmax_step(k, vt, q, None, *carry, acc_ref)

        _, l = lax.fori_loop(0, n_kv, body, (jnp.full((1, tq), -jnp.inf, F32), zer))
        finish(l)


def _attn_specs(s, tq):
    q_spec = pl.BlockSpec((1, HEAD_LANES, tq), lambda bi, h, i: (bi, h, i))
    k_spec = pl.BlockSpec((1, s, HEAD_LANES), lambda bi, h, i: (bi, 0, h))
    v_spec = pl.BlockSpec((1, HEAD_LANES, s), lambda bi, h, i: (bi, h, 0))
    o_spec = pl.BlockSpec((1, tq, HEAD_LANES), lambda bi, h, i: (bi, i, h))
    return q_spec, k_spec, v_spec, o_spec


def _diff_attn(scal, qt, k, vt, g_col, *, tq, tk):
    b, hw, s = qt.shape
    q_spec, k_spec, v_spec, o_spec = _attn_specs(s, tq)
    return pl.pallas_call(
        functools.partial(_diff_attn_kernel, tq=tq, tk=tk, n_kv=s // tk),
        grid=(b, N_DIFF_HEADS, s // tq),
        in_specs=[pl.BlockSpec(memory_space=pltpu.SMEM), q_spec, k_spec, v_spec,
                  pl.BlockSpec(g_col.shape, lambda bi, h, i: (0, 0))],
        out_specs=o_spec,
        out_shape=jax.ShapeDtypeStruct((b, s, hw), BF16),
        scratch_shapes=[pltpu.VMEM((HEAD_LANES, tq), F32), pltpu.VMEM((HEAD_LANES, tq), F32),
                        pltpu.VMEM((tk, tq), F32)],
        compiler_params=pltpu.CompilerParams(
            dimension_semantics=("parallel", "parallel", "arbitrary"), vmem_limit_bytes=V7X_VMEM_LIMIT),
        name="diff_attn",
    )(scal, qt, k, vt, g_col)


def _mla_attn(scal, qt, k, vt, *, tq, tk):
    b, hw, s = qt.shape
    q_spec, k_spec, v_spec, o_spec = _attn_specs(s, tq)
    return pl.pallas_call(
        functools.partial(_mla_attn_kernel, tq=tq, tk=tk, n_kv=s // tk),
        grid=(b, N_MLA_HEADS, s // tq),
        in_specs=[pl.BlockSpec(memory_space=pltpu.SMEM), q_spec, k_spec, v_spec],
        out_specs=o_spec,
        out_shape=jax.ShapeDtypeStruct((b, s, hw), BF16),
        scratch_shapes=[pltpu.VMEM((HEAD_LANES, tq), F32)],
        compiler_params=pltpu.CompilerParams(
            dimension_semantics=("parallel", "parallel", "parallel"), vmem_limit_bytes=V7X_VMEM_LIMIT),
        name="mla_attn",
    )(scal, qt, k, vt)


def _post_attn_kernel(x_ref, od_ref, om_ref, wout_ref, gffn_ref, wqt_ref, keys_ref,
                      x1_ref, xn_ref, sc_ref):
    x1 = (x_ref[0] + _dot(od_ref[0], wout_ref[:DIFF_V_W]) + _dot(om_ref[0], wout_ref[DIFF_V_W:]))
    x1_ref[0] = x1
    ms = jnp.mean(x1 * x1, axis=-1, keepdims=True)
    xn = (x1 * lax.rsqrt(ms + EPS) * gffn_ref[...]).astype(BF16)
    xn_ref[0] = xn
    qt = _nt_dot(wqt_ref[...], xn).astype(BF16)
    for hs in range(2 * PEER_HEADS):
        sc_ref[0, hs] = _dot(keys_ref[hs], qt[hs * PEER_HALF:(hs + 1) * PEER_HALF])


def _post_attn(x, od, om, p, *, tile):
    b, s, d = x.shape
    full = lambda a: pl.BlockSpec(a.shape, lambda bi, i: (0,) * a.ndim)
    tok = lambda w: pl.BlockSpec((1, tile, w), lambda bi, i: (bi, i, 0))
    weights = (p["w_out"], p["gffn"], p["w_qt"], p["keys"])
    return pl.pallas_call(
        _post_attn_kernel,
        grid=(b, s // tile),
        in_specs=[tok(d), tok(DIFF_V_W), tok(MLA_V_W)] + [full(w) for w in weights],
        out_specs=[tok(d), tok(d),
                   pl.BlockSpec((1, 2 * PEER_HEADS, PEER_N_KEYS, tile), lambda bi, i: (bi, 0, 0, i))],
        out_shape=[jax.ShapeDtypeStruct((b, s, d), F32), jax.ShapeDtypeStruct((b, s, d), BF16),
                   jax.ShapeDtypeStruct((b, 2 * PEER_HEADS, PEER_N_KEYS, s), F32)],
        compiler_params=pltpu.CompilerParams(
            dimension_semantics=("parallel", "parallel"), vmem_limit_bytes=V7X_VMEM_LIMIT),
        name="post_attn",
    )(x, od, om, *weights)


_SUB = 8
GELU_HALF = 0.5
LANES = 128
GATE_LANES = 256
HEADS_PER_GROUP = 2
EXPERT_BLOCK = 256


def _top16_ranked(s, sub_idx, row16):
    rank = jnp.full(s.shape, float(PEER_N_KEYS - 1), F32)
    stacked = jnp.zeros(row16.shape, F32)
    vals = []
    for k in range(PEER_TOPK):
        m = jnp.max(s, axis=0, keepdims=True)
        first = jnp.min(jnp.where(s == m, sub_idx, float(PEER_N_KEYS)), axis=0, keepdims=True)
        sel = sub_idx == first
        rank = jnp.where(sel, float(k), rank)
        s = jnp.where(sel, -jnp.inf, s)
        stacked = jnp.where(row16 == float(k), m, stacked)
        vals.append(m)
    return vals, stacked, rank


_CODE_BASE = -(2.0 ** 126)
_CODE_STEP = 2.0 ** 121


def _top16_untied(s, row16):
    stacked = jnp.zeros(row16.shape, F32)
    vals = []
    for k in range(PEER_TOPK):
        m = jnp.max(s, axis=0, keepdims=True)
        s = jnp.where(s == m, _CODE_BASE - k * _CODE_STEP, s)
        stacked = jnp.where(row16 == float(k), m, stacked)
        vals.append(m)
    coded = s <= _CODE_BASE
    rank = jnp.where(coded, s * (-1.0 / _CODE_STEP) - 32.0, float(PEER_N_KEYS - 1))
    n_coded = jnp.sum(jnp.where(coded, 1.0, 0.0), axis=0, keepdims=True)
    return vals, stacked, rank, n_coded


def _peer_topk_kernel(sc_ref, cnt_ref, e1_ref, r2_ref, e2_ref, *, td):
    sub_idx = lax.broadcasted_iota(jnp.int32, (PEER_N_KEYS, td), 0).astype(F32)
    row16 = lax.broadcasted_iota(jnp.int32, (PEER_TOPK, td), 0).astype(F32)
    row8 = lax.broadcasted_iota(jnp.int32, (_SUB, td), 0).astype(F32)
    neg_inf = jnp.full((_SUB, td), -jnp.inf, F32)
    k_top = float(PEER_TOPK)
    col_rows = [min(_SUB, PEER_TOPK // (b + 1)) for b in range(_SUB)]
    col_invalid = functools.reduce(jnp.add, [jnp.where(row8 < float(na), 0.0, 1.0) for na in col_rows])

    def make_head(exact_ties):
        def head(h):
            bad = jnp.zeros((1, td), F32)
            s1 = sc_ref[0, 2 * h]
            s2 = sc_ref[0, 2 * h + 1]
            if exact_ties:
                v1, v1s, rank1 = _top16_ranked(s1, sub_idx, row16)
                v2, v2s, rank2 = _top16_ranked(s2, sub_idx, row16)
            else:
                v1, v1s, rank1, n1 = _top16_untied(s1, row16)
                v2, v2s, rank2, n2 = _top16_untied(s2, row16)
                bad = jnp.maximum(bad, jnp.abs(n1 - k_top) + jnp.abs(n2 - k_top))

            cells = [jnp.where(row8 < float(na), v1s[:_SUB] + v2[b], neg_inf) if na < _SUB
                     else v1s[:_SUB] + v2[b] for b, na in enumerate(col_rows)]
            flats = [row8 * k_top + float(b) for b in range(_SUB)]
            cells.append(v1s[_SUB:] + v2[0]); flats.append((row8 + float(_SUB)) * k_top)
            cells.append(v1[0] + v2s[_SUB:]); flats.append(row8 + float(_SUB))
            big = float(PEER_TOPK * PEER_TOPK)
            top = None
            z = jnp.zeros((1, td), F32)
            for _ in range(PEER_TOPK):
                m = jnp.max(functools.reduce(jnp.maximum, cells), axis=0, keepdims=True)
                if exact_ties:
                    cand = [jnp.where(c == m, f, big) for c, f in zip(cells, flats)]
                    first = jnp.min(functools.reduce(jnp.minimum, cand), axis=0, keepdims=True)
                    cells = [jnp.where(f == first, -jnp.inf, c) for c, f in zip(cells, flats)]
                else:
                    cells = [jnp.where(c == m, -jnp.inf, c) for c in cells]
                if top is None:
                    top = m
                z = z + jnp.exp(m - top)
            knocked = [jnp.where(c == -jnp.inf, 1.0, 0.0) for c in cells]
            tail0 = jnp.sum(knocked[_SUB + 1], axis=0, keepdims=True)
            count_lo = functools.reduce(jnp.add, knocked[:_SUB]) - col_invalid + jnp.where(row8 == 0.0, tail0, 0.0)
            count_hi = knocked[_SUB]
            if not exact_ties:
                total = jnp.sum(count_lo + count_hi, axis=0, keepdims=True)
                bad = jnp.maximum(bad, jnp.abs(total - k_top))

            cnt = jnp.zeros((PEER_N_KEYS, td), F32)
            for a in range(PEER_TOPK):
                src = count_lo if a < _SUB else count_hi
                cnt = jnp.where(rank1 == float(a), src[a % _SUB:a % _SUB + 1], cnt)
            e1 = jnp.exp(s1 - v1[0])
            e2 = jnp.exp(s2 - v2[0]) * (GELU_HALF / z)
            for lt in range(td // LANES):
                tok = slice(lt * LANES, (lt + 1) * LANES)
                cnt_ref[0, lt, h] = cnt[:, tok]
                e1_ref[0, lt, h] = e1[:, tok]
            r2_ref[0, h] = rank2.astype(BF16)
            e2_ref[0, h] = e2.astype(BF16)
            return bad

        return head

    untied_head = make_head(False)
    exact_head = make_head(True)

    def head_group(g, carry):
        heads = [g * HEADS_PER_GROUP + k for k in range(HEADS_PER_GROUP)]
        bads = [untied_head(h) for h in heads]

        for h, bad in zip(heads, bads):
            @pl.when(jnp.max(bad) > 0.0)
            def _():
                exact_head(h)

        return carry

    lax.fori_loop(0, PEER_HEADS // HEADS_PER_GROUP, head_group, 0)


def _peer_topk(sc, *, tile):
    b, _, _, s = sc.shape
    row_spec = pl.BlockSpec((1, tile // LANES, PEER_HEADS, PEER_N_KEYS, LANES), lambda bi, i: (bi, i, 0, 0, 0))
    row_shape = jax.ShapeDtypeStruct((b, s // LANES, PEER_HEADS, PEER_N_KEYS, LANES), F32)
    key_spec = pl.BlockSpec((1, PEER_HEADS, PEER_N_KEYS, tile), lambda bi, i: (bi, 0, 0, i))
    key_shape = jax.ShapeDtypeStruct((b, PEER_HEADS, PEER_N_KEYS, s), BF16)
    return pl.pallas_call(
        functools.partial(_peer_topk_kernel, td=tile),
        grid=(b, s // tile),
        in_specs=[pl.BlockSpec((1, 2 * PEER_HEADS, PEER_N_KEYS, tile), lambda bi, i: (bi, 0, 0, i))],
        out_specs=[row_spec, row_spec, key_spec, key_spec],
        out_shape=[row_shape, row_shape, key_shape, key_shape],
        compiler_params=pltpu.CompilerParams(
            dimension_semantics=("parallel", "parallel"), vmem_limit_bytes=V7X_VMEM_LIMIT),
        name="peer_topk",
    )(sc)


def _gelu_twice(x):
    return x + x * lax.erf(x * (1.0 / math.sqrt(2.0)))


def _packed_row(ref, lane_tiles, h, i1):
    rows = [jnp.broadcast_to(ref[0, lt, h, pl.ds(i1, 1), :], (PEER_N_KEYS, LANES)) for lt in lane_tiles]
    return jnp.concatenate(rows, axis=1).astype(BF16)


def _peer_ffn_kernel(xn_ref, u_ref, vt_ref, cnt_ref, e1_ref, r2_ref, e2_ref, x1_ref, o_ref,
                     acc_ref, gh_ref, *, rows_per_step):
    j = pl.program_id(2)

    @pl.when(j == 0)
    def _():
        acc_ref[...] = jnp.zeros_like(acc_ref)

    tt = xn_ref.shape[1]
    zero = jnp.zeros((PEER_N_KEYS, GATE_LANES), BF16)
    xn = xn_ref[0]
    rows_per_block = EXPERT_BLOCK // PEER_N_KEYS
    for blk in range(rows_per_step // rows_per_block):
        h_tm = _nt_dot(xn, u_ref[blk * EXPERT_BLOCK:(blk + 1) * EXPERT_BLOCK, :])
        act = _gelu_twice(h_tm.T.astype(BF16))
        for rr in range(rows_per_block):
            r = blk * rows_per_block + rr
            i1 = j * rows_per_step + r
            keys = slice(r * PEER_N_KEYS, (r + 1) * PEER_N_KEYS)
            for c in range(tt // GATE_LANES):
                tok = slice(c * GATE_LANES, (c + 1) * GATE_LANES)
                lane_tiles = range(c * GATE_LANES // LANES, (c + 1) * GATE_LANES // LANES)
                gate = zero
                for h in range(PEER_HEADS):
                    count = _packed_row(cnt_ref, lane_tiles, h, i1)
                    e1 = _packed_row(e1_ref, lane_tiles, h, i1)
                    gate = gate + jnp.where(r2_ref[0, h, :, tok] < count, e2_ref[0, h, :, tok], zero) * e1
                gh_ref[keys, tok] = gate * act[rr * PEER_N_KEYS:(rr + 1) * PEER_N_KEYS, tok]
    acc_ref[...] += _dot(vt_ref[...], gh_ref[...])

    @pl.when(j == pl.num_programs(2) - 1)
    def _():
        o_ref[0] = x1_ref[0] + acc_ref[...].T


def _peer_ffn(xn, x1, tables, u_bf, vt_bf, *, tile, experts_per_step):
    b, s, d = x1.shape
    cnt, e1, r2, e2 = tables
    rows = experts_per_step // PEER_N_KEYS
    tok = pl.BlockSpec((1, tile, d), lambda bi, i, j: (bi, i, 0))
    row_tab = pl.BlockSpec((1, tile // LANES, PEER_HEADS, PEER_N_KEYS, LANES), lambda bi, i, j: (bi, i, 0, 0, 0))
    key_tab = pl.BlockSpec((1, PEER_HEADS, PEER_N_KEYS, tile), lambda bi, i, j: (bi, 0, 0, i))
    return pl.pallas_call(
        functools.partial(_peer_ffn_kernel, rows_per_step=rows),
        grid=(b, s // tile, PEER_N_EXPERTS // experts_per_step),
        in_specs=[tok,
                  pl.BlockSpec((experts_per_step, d), lambda bi, i, j: (j, 0)),
                  pl.BlockSpec((d, experts_per_step), lambda bi, i, j: (0, j)),
                  row_tab, row_tab, key_tab, key_tab, tok],
        out_specs=tok,
        out_shape=jax.ShapeDtypeStruct((b, s, d), F32),
        scratch_shapes=[pltpu.VMEM((d, tile), F32), pltpu.VMEM((experts_per_step, tile), BF16)],
        compiler_params=pltpu.CompilerParams(
            dimension_semantics=("parallel", "parallel", "arbitrary"), vmem_limit_bytes=V7X_VMEM_LIMIT),
        name="peer_ffn",
    )(xn, u_bf, vt_bf, cnt, e1, r2, e2, x1)


def _col(v, reps=1, scale=1.0):
    return (jnp.tile(v.astype(F32), reps) * scale).reshape(-1, 1)


def _score_bound(scale_dim, gq, gk):
    bound = scale_dim * jnp.max(jnp.abs(gq.astype(F32))) * jnp.max(jnp.abs(gk.astype(F32))) * BF16_ROUNDING_MARGIN
    return (bound <= SCORE_BOUND_LOG2).astype(F32)


def _layer_params(l, norm_mix_g, w_in, diff_q_norm_g, diff_k_norm_g, lam_q1, lam_k1, lam_q2, lam_k2,
                  diff_subln_g, mla_q_latent_g, mla_w_uq, mla_kv_latent_g, mla_w_ukv, mla_q_norm_g,
                  mla_k_norm_g, w_out, norm_ffn_g, peer_w_q, peer_key1, peer_key2, peer_u, peer_v):
    lam_init = 0.8 - 0.6 * math.exp(-0.3 * l)
    lam = (jnp.exp(jnp.sum(lam_q1.astype(F32) * lam_k1.astype(F32)))
           - jnp.exp(jnp.sum(lam_q2.astype(F32) * lam_k2.astype(F32))) + lam_init)
    slopes = LOG2E * 2.0 ** (-8.0 * jnp.arange(1, N_DIFF_HEADS + 1, dtype=F32) / N_DIFF_HEADS)
    diff_scale = LOG2E * DIFF_QK_DIM ** -0.5
    mla_scale = LOG2E * MLA_QK_DIM ** -0.5
    diff_bound = _score_bound(diff_scale * DIFF_QK_DIM, diff_q_norm_g, diff_k_norm_g)
    mla_bound = _score_bound(mla_scale * MLA_QK_DIM, mla_q_norm_g, mla_k_norm_g)
    scal = jnp.concatenate([jnp.stack([lam, jnp.asarray(1.0 - lam_init, F32), diff_bound]), slopes]).astype(F32)
    keys = jnp.stack([peer_key1, peer_key2], axis=1).reshape(2 * PEER_HEADS, PEER_N_KEYS, PEER_HALF)
    return dict(
        gmix=norm_mix_g.reshape(1, -1), w_int=w_in.T.astype(BF16),
        gq=_col(diff_q_norm_g, 2 * N_DIFF_HEADS, diff_scale), gk=_col(diff_k_norm_g, 2 * N_DIFF_HEADS),
        gcq=_col(mla_q_latent_g), w_uqt=mla_w_uq.T.astype(BF16),
        gckv=_col(mla_kv_latent_g), w_ukvt=mla_w_ukv.T.astype(BF16),
        gmq=_col(mla_q_norm_g, 1, mla_scale), gmk=_col(mla_k_norm_g),
        scal=scal, mla_scal=mla_bound.reshape(1), gsub=_col(diff_subln_g),
        w_out=w_out.astype(BF16), gffn=norm_ffn_g.reshape(1, -1), w_qt=peer_w_q.T.astype(BF16),
        keys=keys.astype(BF16), u=peer_u.astype(BF16), vt=peer_v.T.astype(BF16),
    )


def _rope_tables(s):
    half = MLA_ROPE_DIM // 2
    inv = ROPE_THETA ** (-jnp.arange(half, dtype=F32) / half)
    ang = inv[:, None] * jnp.arange(s, dtype=F32)[None, :]
    return jnp.cos(ang), jnp.sin(ang)


def _tiles(s):
    return dict(proj=min(512, s), tq=min(512, s), tk_diff=2048 if s >= 4096 else min(1024, s), tk_mla=min(2048, s), topk=min(256, s), ffn=min(512, s))


def _layer(x, p):
    s = x.shape[1]
    t = _tiles(s)
    cos_t, sin_t = _rope_tables(s)
    qd, kd, vd, qm, km, vm = _pre_attn(x, p, cos_t, sin_t, tile=t["proj"])
    od = _diff_attn(p["scal"], qd, kd, vd, p["gsub"], tq=t["tq"], tk=t["tk_diff"])
    om = _mla_attn(p["mla_scal"], qm, km, vm, tq=t["tq"], tk=t["tk_mla"])
    x1, xn, sc = _post_attn(x, od, om, p, tile=t["proj"])
    tables = _peer_topk(sc, tile=t["topk"])
    return _peer_ffn(xn, x1, tables, p["u"], p["vt"], tile=t["ffn"], experts_per_step=2048)


def kernel(x_prompt, x_sample, norm_mix_g, w_in, diff_q_norm_g, diff_k_norm_g, lam_q1, lam_k1, lam_q2, lam_k2,
           diff_subln_g, mla_q_latent_g, mla_w_uq, mla_kv_latent_g, mla_w_ukv, mla_q_norm_g, mla_k_norm_g,
           w_out, norm_ffn_g, peer_w_q, peer_key1, peer_key2, peer_u, peer_v):
    stacked = (norm_mix_g, w_in, diff_q_norm_g, diff_k_norm_g, lam_q1, lam_k1, lam_q2, lam_k2, diff_subln_g,
               mla_q_latent_g, mla_w_uq, mla_kv_latent_g, mla_w_ukv, mla_q_norm_g, mla_k_norm_g, w_out,
               norm_ffn_g, peer_w_q, peer_key1, peer_key2, peer_u, peer_v)
    y_prompt, y_sample = x_prompt, x_sample
    for l in range(DEPTH):
        p = _layer_params(l, *(w[l] for w in stacked))
        y_prompt = _layer(y_prompt, p)
        y_sample = _layer(y_sample, p)
    return (y_prompt, y_sample)
```

```python
import functools
import math

import jax
import jax.numpy as jnp
from jax import lax
from jax.experimental import pallas as pl
from jax.experimental.pallas import tpu as pltpu

F32 = jnp.float32
BF16 = jnp.bfloat16

D_MODEL = 1024
DEPTH = 2
N_DIFF_HEADS = 4
DIFF_QK_DIM = 64
DIFF_V_DIM = 128
N_MLA_HEADS = 4
MLA_Q_RANK = 256
MLA_KV_RANK = 256
MLA_NOPE_DIM = 64
MLA_ROPE_DIM = 32
MLA_QK_DIM = MLA_NOPE_DIM + MLA_ROPE_DIM
MLA_V_DIM = 128
ROPE_THETA = 10000.0
DIFF_Q_W = N_DIFF_HEADS * 2 * DIFF_QK_DIM
DIFF_V_W = N_DIFF_HEADS * DIFF_V_DIM
MLA_V_W = N_MLA_HEADS * MLA_V_DIM
IN_WIDTH = 3 * DIFF_Q_W + MLA_Q_RANK + MLA_KV_RANK + MLA_ROPE_DIM
PEER_HEADS = 8
PEER_N_KEYS = 128
PEER_N_EXPERTS = PEER_N_KEYS * PEER_N_KEYS
PEER_HALF = 128
PEER_TOPK = 16
EPS = 1e-6

LOG2E = 1.0 / math.log(2.0)
SCORE_BOUND_LOG2 = 60.0
BF16_ROUNDING_MARGIN = 1.02

KV_UNROLL = 4

HEAD_LANES = 128
V7X_VMEM_LIMIT = 56 * 1024 * 1024

_OFF_DQ = 0
_OFF_DK = DIFF_Q_W
_OFF_DV = 2 * DIFF_Q_W
_OFF_CQ = 3 * DIFF_Q_W
_OFF_CKV = _OFF_CQ + MLA_Q_RANK
_OFF_KR = _OFF_CKV + MLA_KV_RANK


def _nt_dot(a, b):
    return lax.dot_general(a, b, (((1,), (1,)), ((), ())), preferred_element_type=F32)


def _dot(a, b):
    return jnp.dot(a, b, preferred_element_type=F32)


def _group_rmsnorm_fm(a, n_groups, width, g_col):
    t = a.shape[1]
    a3 = a.reshape(n_groups, width, t)
    ms = jnp.mean(a3 * a3, axis=1, keepdims=True)
    return (a3 * lax.rsqrt(ms + EPS)).reshape(n_groups * width, t) * g_col


def _rope_fm(r, cos, sin):
    half = MLA_ROPE_DIM // 2
    r1, r2 = r[:half], r[half:]
    return r1 * cos - r2 * sin, r2 * cos + r1 * sin


def _pre_attn_kernel(x_ref, gmix_ref, wint_ref, gq_ref, gk_ref, gcq_ref, wuqt_ref, gckv_ref,
                     wukvt_ref, gmq_ref, gmk_ref, cos_ref, sin_ref,
                     qd_ref, kd_ref, vd_ref, qm_ref, km_ref, vm_ref, kt_scr):
    x = x_ref[0]
    t = x.shape[0]
    ms = jnp.mean(x * x, axis=-1, keepdims=True)
    xn = (x * lax.rsqrt(ms + EPS) * gmix_ref[...]).astype(BF16)
    ht = _nt_dot(wint_ref[...], xn)
    cos = cos_ref[...]
    sin = sin_ref[...]

    dq = _group_rmsnorm_fm(ht[_OFF_DQ:_OFF_DQ + DIFF_Q_W], 2 * N_DIFF_HEADS, DIFF_QK_DIM, gq_ref[...])
    qd_ref[0] = dq.astype(BF16)
    dk = _group_rmsnorm_fm(ht[_OFF_DK:_OFF_DK + DIFF_Q_W], 2 * N_DIFF_HEADS, DIFF_QK_DIM, gk_ref[...])
    kd_ref[0] = dk.T.astype(BF16)
    vd_ref[0] = ht[_OFF_DV:_OFF_DV + DIFF_V_W].astype(BF16)

    cq = ht[_OFF_CQ:_OFF_CQ + MLA_Q_RANK]
    cqn = (cq * lax.rsqrt(jnp.mean(cq * cq, axis=0, keepdims=True) + EPS) * gcq_ref[...]).astype(BF16)
    mq = _dot(wuqt_ref[...], cqn)
    pad = jnp.zeros((HEAD_LANES - MLA_QK_DIM, t), BF16)
    for h in range(N_MLA_HEADS):
        m = mq[h * MLA_QK_DIM:(h + 1) * MLA_QK_DIM]
        mn = m * lax.rsqrt(jnp.mean(m * m, axis=0, keepdims=True) + EPS) * gmq_ref[...]
        o1, o2 = _rope_fm(mn[MLA_NOPE_DIM:], cos, sin)
        base = h * HEAD_LANES
        qm_ref[0, base:base + MLA_NOPE_DIM] = mn[:MLA_NOPE_DIM].astype(BF16)
        qm_ref[0, base + MLA_NOPE_DIM:base + MLA_NOPE_DIM + 16] = o1.astype(BF16)
        qm_ref[0, base + MLA_NOPE_DIM + 16:base + MLA_QK_DIM] = o2.astype(BF16)
        qm_ref[0, base + MLA_QK_DIM:base + HEAD_LANES] = pad

    ckv = ht[_OFF_CKV:_OFF_CKV + MLA_KV_RANK]
    ckvn = (ckv * lax.rsqrt(jnp.mean(ckv * ckv, axis=0, keepdims=True) + EPS) * gckv_ref[...]).astype(BF16)
    kv = _dot(wukvt_ref[...], ckvn)
    kr = ht[_OFF_KR:_OFF_KR + MLA_ROPE_DIM]
    kr_ss = jnp.sum(kr * kr, axis=0, keepdims=True)
    gmk = gmk_ref[...]
    per_head = MLA_NOPE_DIM + MLA_V_DIM
    for h in range(N_MLA_HEADS):
        kn = kv[h * per_head:h * per_head + MLA_NOPE_DIM]
        v = kv[h * per_head + MLA_NOPE_DIM:(h + 1) * per_head]
        ms_k = (jnp.sum(kn * kn, axis=0, keepdims=True) + kr_ss) * (1.0 / MLA_QK_DIM)
        inv = lax.rsqrt(ms_k + EPS)
        o1, o2 = _rope_fm(kr * inv * gmk[MLA_NOPE_DIM:], cos, sin)
        base = h * HEAD_LANES
        kt_scr[base:base + MLA_NOPE_DIM] = kn * inv * gmk[:MLA_NOPE_DIM]
        kt_scr[base + MLA_NOPE_DIM:base + MLA_NOPE_DIM + 16] = o1
        kt_scr[base + MLA_NOPE_DIM + 16:base + MLA_QK_DIM] = o2
        kt_scr[base + MLA_QK_DIM:base + HEAD_LANES] = jnp.zeros((HEAD_LANES - MLA_QK_DIM, t), F32)
        vm_ref[0, base:base + HEAD_LANES] = v.astype(BF16)
    km_ref[0] = kt_scr[...].T.astype(BF16)


def _pre_attn(x, p, cos_t, sin_t, *, tile):
    b, s, d = x.shape
    hw = N_DIFF_HEADS * HEAD_LANES
    full = lambda a: pl.BlockSpec(a.shape, lambda bi, i: (0,) * a.ndim)
    fm_spec = pl.BlockSpec((1, hw, tile), lambda bi, i: (bi, 0, i))
    tm_spec = pl.BlockSpec((1, tile, hw), lambda bi, i: (bi, i, 0))
    weights = (p["gmix"], p["w_int"], p["gq"], p["gk"], p["gcq"], p["w_uqt"], p["gckv"], p["w_ukvt"],
               p["gmq"], p["gmk"])
    rope_spec = pl.BlockSpec((MLA_ROPE_DIM // 2, tile), lambda bi, i: (0, i))
    fm = jax.ShapeDtypeStruct((b, hw, s), BF16)
    tm = jax.ShapeDtypeStruct((b, s, hw), BF16)
    return pl.pallas_call(
        _pre_attn_kernel,
        grid=(b, s // tile),
        in_specs=[pl.BlockSpec((1, tile, d), lambda bi, i: (bi, i, 0))] + [full(w) for w in weights]
        + [rope_spec, rope_spec],
        out_specs=[fm_spec, tm_spec, fm_spec, fm_spec, tm_spec, fm_spec],
        out_shape=[fm, tm, fm, fm, tm, fm],
        scratch_shapes=[pltpu.VMEM((hw, tile), F32)],
        compiler_params=pltpu.CompilerParams(
            dimension_semantics=("parallel", "parallel"), vmem_limit_bytes=V7X_VMEM_LIMIT),
        name="pre_attn",
    )(x, *weights, cos_t, sin_t)


def _online_softmax_step(k, vt, qp, bias, m, l, acc_ref):
    s = _dot(k, qp)
    if bias is not None:
        s = s - bias
    mn = jnp.maximum(m, jnp.max(s, axis=0, keepdims=True))
    alpha = jnp.exp2(m - mn)
    p = jnp.exp2(s - mn)
    l = alpha * l + jnp.sum(p, axis=0, keepdims=True)
    acc_ref[...] = alpha * acc_ref[...] + _dot(vt, p.astype(BF16))
    return mn, l


def _bounded_softmax_step(k, vt, qp, bias, l, acc_ref):
    s = _dot(k, qp)
    if bias is not None:
        s = s - bias
    p = jnp.exp2(s)
    acc_ref[...] += _dot(vt, p.astype(BF16))
    return l + jnp.sum(p, axis=0, keepdims=True)


def _kv_chunk(k_ref, vt_ref, j, tk):
    off = pl.multiple_of(j * tk, tk)
    return k_ref[0, pl.ds(off, tk), :], vt_ref[0, :, pl.ds(off, tk)]


def _diff_attn_kernel(scal_ref, qt_ref, k_ref, vt_ref, g_ref, o_ref, acc0_ref, acc1_ref, rel_ref, *, tq, tk, n_kv):
    h = pl.program_id(1)
    i = pl.program_id(2)
    lam = scal_ref[0]
    sub_scale = scal_ref[1]
    bounded = scal_ref[2] > 0.5
    slope = scal_ref[3 + h]
    q = qt_ref[0]
    row = lax.broadcasted_iota(jnp.int32, q.shape, 0)
    zero = jnp.zeros_like(q)
    q0 = jnp.where(row < DIFF_QK_DIM, q, zero)
    q1 = jnp.where(row >= DIFF_QK_DIM, q, zero)
    @pl.when(i == 0)
    def _():
        rel_ref[...] = (lax.broadcasted_iota(jnp.int32, (tk, tq), 1)
                        - lax.broadcasted_iota(jnp.int32, (tk, tq), 0)).astype(F32)

    def alibi_bias(j):
        shift = lax.convert_element_type(i * tq - j * tk, F32)
        return slope * jnp.abs(rel_ref[...] + shift)
    acc0_ref[...] = jnp.zeros_like(acc0_ref)
    acc1_ref[...] = jnp.zeros_like(acc1_ref)

    neg = jnp.full((1, tq), -jnp.inf, F32)
    zer = jnp.zeros((1, tq), F32)

    def finish(l0, l1):
        o = acc0_ref[...] / l0 - lam * (acc1_ref[...] / l1)
        o = o * lax.rsqrt(jnp.mean(o * o, axis=0, keepdims=True) + EPS) * g_ref[...] * sub_scale
        o_ref[0] = o.T.astype(BF16)

    @pl.when(bounded)
    def _():
        def body(j, carry):
            l0, l1 = carry
            k, vt = _kv_chunk(k_ref, vt_ref, j, tk)
            bias = alibi_bias(j)
            l0 = _bounded_softmax_step(k, vt, q0, bias, l0, acc0_ref)
            l1 = _bounded_softmax_step(k, vt, q1, bias, l1, acc1_ref)
            return l0, l1

        finish(*lax.fori_loop(0, n_kv, body, (zer, zer), unroll=KV_UNROLL))

    @pl.when(jnp.logical_not(bounded))
    def _():
        def body(j, carry):
            m0, l0, m1, l1 = carry
            k, vt = _kv_chunk(k_ref, vt_ref, j, tk)
            bias = alibi_bias(j)
            m0, l0 = _online_softmax_step(k, vt, q0, bias, m0, l0, acc0_ref)
            m1, l1 = _online_softmax_step(k, vt, q1, bias, m1, l1, acc1_ref)
            return m0, l0, m1, l1

        _, l0, _, l1 = lax.fori_loop(0, n_kv, body, (neg, zer, neg, zer))
        finish(l0, l1)


def _mla_attn_kernel(scal_ref, qt_ref, k_ref, vt_ref, o_ref, acc_ref, *, tq, tk, n_kv):
    bounded = scal_ref[0] > 0.5
    q = qt_ref[0]
    acc_ref[...] = jnp.zeros_like(acc_ref)
    zer = jnp.zeros((1, tq), F32)

    def finish(l):
        o_ref[0] = (acc_ref[...] / l).T.astype(BF16)

    @pl.when(bounded)
    def _():
        def body(j, l):
            k, vt = _kv_chunk(k_ref, vt_ref, j, tk)
            return _bounded_softmax_step(k, vt, q, None, l, acc_ref)

        finish(lax.fori_loop(0, n_kv, body, zer, unroll=KV_UNROLL))

    @pl.when(jnp.logical_not(bounded))
    def _():
        def body(j, carry):
            k, vt = _kv_chunk(k_ref, vt_ref, j, tk)
            return _online_softmax_step(k, vt, q, None, *carry, acc_ref)

        _, l = lax.fori_loop(0, n_kv, body, (jnp.full((1, tq), -jnp.inf, F32), zer))
        finish(l)


def _attn_specs(s, tq):
    q_spec = pl.BlockSpec((1, HEAD_LANES, tq), lambda bi, h, i: (bi, h, i))
    k_spec = pl.BlockSpec((1, s, HEAD_LANES), lambda bi, h, i: (bi, 0, h))
    v_spec = pl.BlockSpec((1, HEAD_LANES, s), lambda bi, h, i: (bi, h, 0))
    o_spec = pl.BlockSpec((1, tq, HEAD_LANES), lambda bi, h, i: (bi, i, h))
    return q_spec, k_spec, v_spec, o_spec


def _diff_attn(scal, qt, k, vt, g_col, *, tq, tk):
    b, hw, s = qt.shape
    q_spec, k_spec, v_spec, o_spec = _attn_specs(s, tq)
    return pl.pallas_call(
        functools.partial(_diff_attn_kernel, tq=tq, tk=tk, n_kv=s // tk),
        grid=(b, N_DIFF_HEADS, s // tq),
        in_specs=[pl.BlockSpec(memory_space=pltpu.SMEM), q_spec, k_spec, v_spec,
                  pl.BlockSpec(g_col.shape, lambda bi, h, i: (0, 0))],
        out_specs=o_spec,
        out_shape=jax.ShapeDtypeStruct((b, s, hw), BF16),
        scratch_shapes=[pltpu.VMEM((HEAD_LANES, tq), F32), pltpu.VMEM((HEAD_LANES, tq), F32),
                        pltpu.VMEM((tk, tq), F32)],
        compiler_params=pltpu.CompilerParams(
            dimension_semantics=("parallel", "parallel", "arbitrary"), vmem_limit_bytes=V7X_VMEM_LIMIT),
        name="diff_attn",
    )(scal, qt, k, vt, g_col)


def _mla_attn(scal, qt, k, vt, *, tq, tk):
    b, hw, s = qt.shape
    q_spec, k_spec, v_spec, o_spec = _attn_specs(s, tq)
    return pl.pallas_call(
        functools.partial(_mla_attn_kernel, tq=tq, tk=tk, n_kv=s // tk),
        grid=(b, N_MLA_HEADS, s // tq),
        in_specs=[pl.BlockSpec(memory_space=pltpu.SMEM), q_spec, k_spec, v_spec],
        out_specs=o_spec,
        out_shape=jax.ShapeDtypeStruct((b, s, hw), BF16),
        scratch_shapes=[pltpu.VMEM((HEAD_LANES, tq), F32)],
        compiler_params=pltpu.CompilerParams(
            dimension_semantics=("parallel", "parallel", "parallel"), vmem_limit_bytes=V7X_VMEM_LIMIT),
        name="mla_attn",
    )(scal, qt, k, vt)


def _post_attn_kernel(x_ref, od_ref, om_ref, wout_ref, gffn_ref, wqt_ref, keys_ref,
                      x1_ref, xn_ref, sc_ref):
    x1 = (x_ref[0] + _dot(od_ref[0], wout_ref[:DIFF_V_W]) + _dot(om_ref[0], wout_ref[DIFF_V_W:]))
    x1_ref[0] = x1
    ms = jnp.mean(x1 * x1, axis=-1, keepdims=True)
    xn = (x1 * lax.rsqrt(ms + EPS) * gffn_ref[...]).astype(BF16)
    xn_ref[0] = xn
    qt = _nt_dot(wqt_ref[...], xn).astype(BF16)
    for hs in range(2 * PEER_HEADS):
        sc_ref[0, hs] = _dot(keys_ref[hs], qt[hs * PEER_HALF:(hs + 1) * PEER_HALF])


def _post_attn(x, od, om, p, *, tile):
    b, s, d = x.shape
    full = lambda a: pl.BlockSpec(a.shape, lambda bi, i: (0,) * a.ndim)
    tok = lambda w: pl.BlockSpec((1, tile, w), lambda bi, i: (bi, i, 0))
    weights = (p["w_out"], p["gffn"], p["w_qt"], p["keys"])
    return pl.pallas_call(
        _post_attn_kernel,
        grid=(b, s // tile),
        in_specs=[tok(d), tok(DIFF_V_W), tok(MLA_V_W)] + [full(w) for w in weights],
        out_specs=[tok(d), tok(d),
                   pl.BlockSpec((1, 2 * PEER_HEADS, PEER_N_KEYS, tile), lambda bi, i: (bi, 0, 0, i))],
        out_shape=[jax.ShapeDtypeStruct((b, s, d), F32), jax.ShapeDtypeStruct((b, s, d), BF16),
                   jax.ShapeDtypeStruct((b, 2 * PEER_HEADS, PEER_N_KEYS, s), F32)],
        compiler_params=pltpu.CompilerParams(
            dimension_semantics=("parallel", "parallel"), vmem_limit_bytes=V7X_VMEM_LIMIT),
        name="post_attn",
    )(x, od, om, *weights)


_SUB = 8
GELU_HALF = 0.5
LANES = 128
GATE_LANES = 256
HEADS_PER_GROUP = 2


def _top16_ranked(s, sub_idx, row16):
    rank = jnp.full(s.shape, float(PEER_N_KEYS - 1), F32)
    stacked = jnp.zeros(row16.shape, F32)
    vals = []
    for k in range(PEER_TOPK):
        m = jnp.max(s, axis=0, keepdims=True)
        first = jnp.min(jnp.where(s == m, sub_idx, float(PEER_N_KEYS)), axis=0, keepdims=True)
        sel = sub_idx == first
        rank = jnp.where(sel, float(k), rank)
        s = jnp.where(sel, -jnp.inf, s)
        stacked = jnp.where(row16 == float(k), m, stacked)
        vals.append(m)
    return vals, stacked, rank


_CODE_BASE = -(2.0 ** 126)
_CODE_STEP = 2.0 ** 121


def _top16_untied(s, row16):
    stacked = jnp.zeros(row16.shape, F32)
    vals = []
    for k in range(PEER_TOPK):
        m = jnp.max(s, axis=0, keepdims=True)
        s = jnp.where(s == m, _CODE_BASE - k * _CODE_STEP, s)
        stacked = jnp.where(row16 == float(k), m, stacked)
        vals.append(m)
    coded = s <= _CODE_BASE
    rank = jnp.where(coded, s * (-1.0 / _CODE_STEP) - 32.0, float(PEER_N_KEYS - 1))
    n_coded = jnp.sum(jnp.where(coded, 1.0, 0.0), axis=0, keepdims=True)
    return vals, stacked, rank, n_coded


def _peer_topk_kernel(sc_ref, cnt_ref, e1_ref, r2_ref, e2_ref, *, td):
    sub_idx = lax.broadcasted_iota(jnp.int32, (PEER_N_KEYS, td), 0).astype(F32)
    row16 = lax.broadcasted_iota(jnp.int32, (PEER_TOPK, td), 0).astype(F32)
    row8 = lax.broadcasted_iota(jnp.int32, (_SUB, td), 0).astype(F32)
    neg_inf = jnp.full((_SUB, td), -jnp.inf, F32)
    k_top = float(PEER_TOPK)
    col_rows = [min(_SUB, PEER_TOPK // (b + 1)) for b in range(_SUB)]
    col_invalid = functools.reduce(jnp.add, [jnp.where(row8 < float(na), 0.0, 1.0) for na in col_rows])

    def make_head(exact_ties):
        def head(h):
            bad = jnp.zeros((1, td), F32)
            s1 = sc_ref[0, 2 * h]
            s2 = sc_ref[0, 2 * h + 1]
            if exact_ties:
                v1, v1s, rank1 = _top16_ranked(s1, sub_idx, row16)
                v2, v2s, rank2 = _top16_ranked(s2, sub_idx, row16)
            else:
                v1, v1s, rank1, n1 = _top16_untied(s1, row16)
                v2, v2s, rank2, n2 = _top16_untied(s2, row16)
                bad = jnp.maximum(bad, jnp.abs(n1 - k_top) + jnp.abs(n2 - k_top))

            cells = [jnp.where(row8 < float(na), v1s[:_SUB] + v2[b], neg_inf) if na < _SUB
                     else v1s[:_SUB] + v2[b] for b, na in enumerate(col_rows)]
            flats = [row8 * k_top + float(b) for b in range(_SUB)]
            cells.append(v1s[_SUB:] + v2[0]); flats.append((row8 + float(_SUB)) * k_top)
            cells.append(v1[0] + v2s[_SUB:]); flats.append(row8 + float(_SUB))
            big = float(PEER_TOPK * PEER_TOPK)
            top = None
            z = jnp.zeros((1, td), F32)
            for _ in range(PEER_TOPK):
                m = jnp.max(functools.reduce(jnp.maximum, cells), axis=0, keepdims=True)
                if exact_ties:
                    cand = [jnp.where(c == m, f, big) for c, f in zip(cells, flats)]
                    first = jnp.min(functools.reduce(jnp.minimum, cand), axis=0, keepdims=True)
                    cells = [jnp.where(f == first, -jnp.inf, c) for c, f in zip(cells, flats)]
                else:
                    cells = [jnp.where(c == m, -jnp.inf, c) for c in cells]
                if top is None:
                    top = m
                z = z + jnp.exp(m - top)
            knocked = [jnp.where(c == -jnp.inf, 1.0, 0.0) for c in cells]
            tail0 = jnp.sum(knocked[_SUB + 1], axis=0, keepdims=True)
            count_lo = functools.reduce(jnp.add, knocked[:_SUB]) - col_invalid + jnp.where(row8 == 0.0, tail0, 0.0)
            count_hi = knocked[_SUB]
            if not exact_ties:
                total = jnp.sum(count_lo + count_hi, axis=0, keepdims=True)
                bad = jnp.maximum(bad, jnp.abs(total - k_top))

            cnt = jnp.zeros((PEER_N_KEYS, td), F32)
            for a in range(PEER_TOPK):
                src = count_lo if a < _SUB else count_hi
                cnt = jnp.where(rank1 == float(a), src[a % _SUB:a % _SUB + 1], cnt)
            e1 = jnp.exp(s1 - v1[0])
            e2 = jnp.exp(s2 - v2[0]) * (GELU_HALF / z)
            for lt in range(td // LANES):
                tok = slice(lt * LANES, (lt + 1) * LANES)
                cnt_ref[0, lt, h] = cnt[:, tok]
                e1_ref[0, lt, h] = e1[:, tok]
            r2_ref[0, h] = rank2.astype(BF16)
            e2_ref[0, h] = e2.astype(BF16)
            return bad

        return head

    untied_head = make_head(False)
    exact_head = make_head(True)

    def head_group(g, carry):
        heads = [g * HEADS_PER_GROUP + k for k in range(HEADS_PER_GROUP)]
        bads = [untied_head(h) for h in heads]

        for h, bad in zip(heads, bads):
            @pl.when(jnp.max(bad) > 0.0)
            def _():
                exact_head(h)

        return carry

    lax.fori_loop(0, PEER_HEADS // HEADS_PER_GROUP, head_group, 0)


def _peer_topk(sc, *, tile):
    b, _, _, s = sc.shape
    row_spec = pl.BlockSpec((1, tile // LANES, PEER_HEADS, PEER_N_KEYS, LANES), lambda bi, i: (bi, i, 0, 0, 0))
    row_shape = jax.ShapeDtypeStruct((b, s // LANES, PEER_HEADS, PEER_N_KEYS, LANES), F32)
    key_spec = pl.BlockSpec((1, PEER_HEADS, PEER_N_KEYS, tile), lambda bi, i: (bi, 0, 0, i))
    key_shape = jax.ShapeDtypeStruct((b, PEER_HEADS, PEER_N_KEYS, s), BF16)
    return pl.pallas_call(
        functools.partial(_peer_topk_kernel, td=tile),
        grid=(b, s // tile),
        in_specs=[pl.BlockSpec((1, 2 * PEER_HEADS, PEER_N_KEYS, tile), lambda bi, i: (bi, 0, 0, i))],
        out_specs=[row_spec, row_spec, key_spec, key_spec],
        out_shape=[row_shape, row_shape, key_shape, key_shape],
        compiler_params=pltpu.CompilerParams(
            dimension_semantics=("parallel", "parallel"), vmem_limit_bytes=V7X_VMEM_LIMIT),
        name="peer_topk",
    )(sc)


def _gelu_twice(x):
    return x + x * lax.erf(x * (1.0 / math.sqrt(2.0)))


def _packed_row(ref, lane_tiles, h, i1):
    rows = [jnp.broadcast_to(ref[0, lt, h, pl.ds(i1, 1), :], (PEER_N_KEYS, LANES)) for lt in lane_tiles]
    return jnp.concatenate(rows, axis=1).astype(BF16)


def _peer_ffn_kernel(xn_ref, u_ref, vt_ref, cnt_ref, e1_ref, r2_ref, e2_ref, x1_ref, o_ref,
                     acc_ref, gh_ref, *, rows_per_step):
    j = pl.program_id(2)

    @pl.when(j == 0)
    def _():
        acc_ref[...] = jnp.zeros_like(acc_ref)

    tt = xn_ref.shape[1]
    zero = jnp.zeros((PEER_N_KEYS, GATE_LANES), BF16)
    ht = _nt_dot(u_ref[...], xn_ref[0])
    act = _gelu_twice(ht.astype(BF16))
    for r in range(rows_per_step):
        i1 = j * rows_per_step + r
        keys = slice(r * PEER_N_KEYS, (r + 1) * PEER_N_KEYS)
        for c in range(tt // GATE_LANES):
            tok = slice(c * GATE_LANES, (c + 1) * GATE_LANES)
            lane_tiles = range(c * GATE_LANES // LANES, (c + 1) * GATE_LANES // LANES)
            gate = zero
            for h in range(PEER_HEADS):
                count = _packed_row(cnt_ref, lane_tiles, h, i1)
                e1 = _packed_row(e1_ref, lane_tiles, h, i1)
                gate = gate + jnp.where(r2_ref[0, h, :, tok] < count, e2_ref[0, h, :, tok], zero) * e1
            gh_ref[keys, tok] = gate * act[keys, tok]
    acc_ref[...] += _dot(vt_ref[...], gh_ref[...])

    @pl.when(j == pl.num_programs(2) - 1)
    def _():
        o_ref[0] = x1_ref[0] + acc_ref[...].T


def _peer_ffn(xn, x1, tables, u_bf, vt_bf, *, tile, experts_per_step):
    b, s, d = x1.shape
    cnt, e1, r2, e2 = tables
    rows = experts_per_step // PEER_N_KEYS
    tok = pl.BlockSpec((1, tile, d), lambda bi, i, j: (bi, i, 0))
    row_tab = pl.BlockSpec((1, tile // LANES, PEER_HEADS, PEER_N_KEYS, LANES), lambda bi, i, j: (bi, i, 0, 0, 0))
    key_tab = pl.BlockSpec((1, PEER_HEADS, PEER_N_KEYS, tile), lambda bi, i, j: (bi, 0, 0, i))
    return pl.pallas_call(
        functools.partial(_peer_ffn_kernel, rows_per_step=rows),
        grid=(b, s // tile, PEER_N_EXPERTS // experts_per_step),
        in_specs=[tok,
                  pl.BlockSpec((experts_per_step, d), lambda bi, i, j: (j, 0)),
                  pl.BlockSpec((d, experts_per_step), lambda bi, i, j: (0, j)),
                  row_tab, row_tab, key_tab, key_tab, tok],
        out_specs=tok,
        out_shape=jax.ShapeDtypeStruct((b, s, d), F32),
        scratch_shapes=[pltpu.VMEM((d, tile), F32), pltpu.VMEM((experts_per_step, tile), BF16)],
        compiler_params=pltpu.CompilerParams(
            dimension_semantics=("parallel", "parallel", "arbitrary"), vmem_limit_bytes=V7X_VMEM_LIMIT),
        name="peer_ffn",
    )(xn, u_bf, vt_bf, cnt, e1, r2, e2, x1)


def _col(v, reps=1, scale=1.0):
    return (jnp.tile(v.astype(F32), reps) * scale).reshape(-1, 1)


def _score_bound(scale_dim, gq, gk):
    bound = scale_dim * jnp.max(jnp.abs(gq.astype(F32))) * jnp.max(jnp.abs(gk.astype(F32))) * BF16_ROUNDING_MARGIN
    return (bound <= SCORE_BOUND_LOG2).astype(F32)


def _layer_params(l, norm_mix_g, w_in, diff_q_norm_g, diff_k_norm_g, lam_q1, lam_k1, lam_q2, lam_k2,
                  diff_subln_g, mla_q_latent_g, mla_w_uq, mla_kv_latent_g, mla_w_ukv, mla_q_norm_g,
                  mla_k_norm_g, w_out, norm_ffn_g, peer_w_q, peer_key1, peer_key2, peer_u, peer_v):
    lam_init = 0.8 - 0.6 * math.exp(-0.3 * l)
    lam = (jnp.exp(jnp.sum(lam_q1.astype(F32) * lam_k1.astype(F32)))
           - jnp.exp(jnp.sum(lam_q2.astype(F32) * lam_k2.astype(F32))) + lam_init)
    slopes = LOG2E * 2.0 ** (-8.0 * jnp.arange(1, N_DIFF_HEADS + 1, dtype=F32) / N_DIFF_HEADS)
    diff_scale = LOG2E * DIFF_QK_DIM ** -0.5
    mla_scale = LOG2E * MLA_QK_DIM ** -0.5
    diff_bound = _score_bound(diff_scale * DIFF_QK_DIM, diff_q_norm_g, diff_k_norm_g)
    mla_bound = _score_bound(mla_scale * MLA_QK_DIM, mla_q_norm_g, mla_k_norm_g)
    scal = jnp.concatenate([jnp.stack([lam, jnp.asarray(1.0 - lam_init, F32), diff_bound]), slopes]).astype(F32)
    keys = jnp.stack([peer_key1, peer_key2], axis=1).reshape(2 * PEER_HEADS, PEER_N_KEYS, PEER_HALF)
    return dict(
        gmix=norm_mix_g.reshape(1, -1), w_int=w_in.T.astype(BF16),
        gq=_col(diff_q_norm_g, 2 * N_DIFF_HEADS, diff_scale), gk=_col(diff_k_norm_g, 2 * N_DIFF_HEADS),
        gcq=_col(mla_q_latent_g), w_uqt=mla_w_uq.T.astype(BF16),
        gckv=_col(mla_kv_latent_g), w_ukvt=mla_w_ukv.T.astype(BF16),
        gmq=_col(mla_q_norm_g, 1, mla_scale), gmk=_col(mla_k_norm_g),
        scal=scal, mla_scal=mla_bound.reshape(1), gsub=_col(diff_subln_g),
        w_out=w_out.astype(BF16), gffn=norm_ffn_g.reshape(1, -1), w_qt=peer_w_q.T.astype(BF16),
        keys=keys.astype(BF16), u=peer_u.astype(BF16), vt=peer_v.T.astype(BF16),
    )


def _rope_tables(s):
    half = MLA_ROPE_DIM // 2
    inv = ROPE_THETA ** (-jnp.arange(half, dtype=F32) / half)
    ang = inv[:, None] * jnp.arange(s, dtype=F32)[None, :]
    return jnp.cos(ang), jnp.sin(ang)


def _tiles(s):
    return dict(proj=min(512, s), tq=min(512, s), tk_diff=2048 if s >= 4096 else min(1024, s), tk_mla=min(2048, s), topk=min(256, s), ffn=min(512, s))


def _layer(x, p):
    s = x.shape[1]
    t = _tiles(s)
    cos_t, sin_t = _rope_tables(s)
    qd, kd, vd, qm, km, vm = _pre_attn(x, p, cos_t, sin_t, tile=t["proj"])
    od = _diff_attn(p["scal"], qd, kd, vd, p["gsub"], tq=t["tq"], tk=t["tk_diff"])
    om = _mla_attn(p["mla_scal"], qm, km, vm, tq=t["tq"], tk=t["tk_mla"])
    x1, xn, sc = _post_attn(x, od, om, p, tile=t["proj"])
    tables = _peer_topk(sc, tile=t["topk"])
    return _peer_ffn(xn, x1, tables, p["u"], p["vt"], tile=t["ffn"], experts_per_step=2048)


def kernel(x_prompt, x_sample, norm_mix_g, w_in, diff_q_norm_g, diff_k_norm_g, lam_q1, lam_k1, lam_q2, lam_k2,
           diff_subln_g, mla_q_latent_g, mla_w_uq, mla_kv_latent_g, mla_w_ukv, mla_q_norm_g, mla_k_norm_g,
           w_out, norm_ffn_g, peer_w_q, peer_key1, peer_key2, peer_u, peer_v):
    stacked = (norm_mix_g, w_in, diff_q_norm_g, diff_k_norm_g, lam_q1, lam_k1, lam_q2, lam_k2, diff_subln_g,
               mla_q_latent_g, mla_w_uq, mla_kv_latent_g, mla_w_ukv, mla_q_norm_g, mla_k_norm_g, w_out,
               norm_ffn_g, peer_w_q, peer_key1, peer_key2, peer_u, peer_v)
    y_prompt, y_sample = x_prompt, x_sample
    for l in range(DEPTH):
        p = _layer_params(l, *(w[l] for w in stacked))
        y_prompt = _layer(y_prompt, p)
        y_sample = _layer(y_sample, p)
    return (y_prompt, y_sample)
```
